```python
import math
import jax, jax.numpy as jnp
from jax import lax
import numpy as np

D_MODEL = 1024
BATCH = 16
SEQ = 256
DEPTH = 2
DEC_BATCH = 2
DEC_SEQ = 2048
PAST_LEN = 256

GRID_W = 64
ROPE_BASE = 10000.0
N_HEADS = 8
N_KV_HEADS = 2
GROUPS = N_HEADS // N_KV_HEADS
HEAD_DIM = 64
WINDOW = 128
BLOCK = 128
Q_DIM = N_HEADS * HEAD_DIM
KV_DIM = N_KV_HEADS * HEAD_DIM
POOL_SIZES = (2, 4, 8, 16)
N_POOL_GROUPS = len(POOL_SIZES)
POOL_DIM = D_MODEL // 2
POOL_GROUP_DIM = POOL_DIM // N_POOL_GROUPS
IN_DIM = Q_DIM + 2 * KV_DIM + POOL_DIM
MIX_OUT = Q_DIM + POOL_DIM
CONV_DIM = D_MODEL
CONV_WIDTH = 31
N_KEYS = 128
N_EXPERTS = N_KEYS * N_KEYS
PEER_HEADS = 8
PEER_QDIM = 256
PEER_HALF = PEER_QDIM // 2
PEER_TOPK = 16
PEER_CHUNK = 128
N_ATTN_LAYERS = (DEPTH + 1) // 2
N_CONV_LAYERS = DEPTH // 2
EPS = 1e-6
NEG = -1e30

kernel_name = "hybrid_prefix_diffusion_step"


def rms_norm(x, g):
    xf = x.astype(jnp.float32)
    y = xf * lax.rsqrt(jnp.mean(xf * xf, axis=-1, keepdims=True) + EPS)
    return (y * g.astype(jnp.float32)).astype(x.dtype)


def layer_norm(x, g, b):
    xf = x.astype(jnp.float32)
    mu = jnp.mean(xf, axis=-1, keepdims=True)
    var = jnp.mean(jnp.square(xf - mu), axis=-1, keepdims=True)
    y = (xf - mu) * lax.rsqrt(var + EPS)
    return (y * g.astype(jnp.float32) + b.astype(jnp.float32)).astype(x.dtype)


def adaln(cvec, w, b):
    m = jax.nn.silu(cvec) @ w + b
    return jnp.split(m[:, None, :], 6, axis=-1)


def modulate(h, shift, scale):
    return h * (1 + scale) + shift


def _rotate(x, ang):
    m = ang.shape[-1]
    cos = jnp.cos(ang)[:, None, :]
    sin = jnp.sin(ang)[:, None, :]
    x1, x2 = x[..., :m], x[..., m:]
    return jnp.concatenate([x1 * cos - x2 * sin, x2 * cos + x1 * sin], axis=-1)


def axial_rope(x):
    S = x.shape[1]
    n_rows = S // GRID_W
    rows = jnp.repeat(jnp.arange(n_rows, dtype=jnp.float32), GRID_W)
    cols = jnp.tile(jnp.arange(GRID_W, dtype=jnp.float32), n_rows)
    half = HEAD_DIM // 2
    quarter = half // 2
    freqs = ROPE_BASE ** (-jnp.arange(quarter, dtype=jnp.float32) / quarter)
    xf = x.astype(jnp.float32)
    xr = _rotate(xf[..., :half], rows[:, None] * freqs[None, :])
    xc = _rotate(xf[..., half:], cols[:, None] * freqs[None, :])
    return jnp.concatenate([xr, xc], axis=-1).astype(x.dtype)


def sink_softmax(logits, sink):
    sk = jnp.broadcast_to(sink.astype(jnp.float32).reshape(N_KV_HEADS, GROUPS)[None, :, :, None, None],
                          logits.shape[:-1] + (1,))
    p = jax.nn.softmax(jnp.concatenate([logits, sk], axis=-1), axis=-1)
    return p[..., :-1]


def context_attention(q, k, v, sink):
    B, S = q.shape[0], q.shape[1]
    nb = S // BLOCK
    scale = HEAD_DIM ** -0.5
    qb = q.reshape(B, nb, BLOCK, N_KV_HEADS, GROUPS, HEAD_DIM).transpose(1, 0, 2, 3, 4, 5)

    def blk(qn):
        s = jnp.einsum('bqkgd,bskd->bkgqs', qn, k).astype(jnp.float32) * scale
        p = sink_softmax(s, sink).astype(v.dtype)
        return jnp.einsum('bkgqs,bskd->bqkgd', p, v)

    o = lax.map(blk, qb)
    return o.transpose(1, 0, 2, 3, 4, 5).reshape(B, S, Q_DIM)


def latent_attention(q, k, v, ck, cv, sink):
    B, S = q.shape[0], q.shape[1]
    nb = S // BLOCK
    P = ck.shape[1]
    scale = HEAD_DIM ** -0.5
    qb = q.reshape(B, nb, BLOCK, N_KV_HEADS, GROUPS, HEAD_DIM).transpose(1, 0, 2, 3, 4, 5)
    pad = ((0, 0), (BLOCK, BLOCK), (0, 0), (0, 0))
    kp = jnp.pad(k, pad).reshape(B, nb + 2, BLOCK, N_KV_HEADS, HEAD_DIM)
    vp = jnp.pad(v, pad).reshape(B, nb + 2, BLOCK, N_KV_HEADS, HEAD_DIM)
    kw = jnp.concatenate([kp[:, :-2], kp[:, 1:-1], kp[:, 2:]], axis=2).transpose(1, 0, 2, 3, 4)
    vw = jnp.concatenate([vp[:, :-2], vp[:, 1:-1], vp[:, 2:]], axis=2).transpose(1, 0, 2, 3, 4)
    qpos = jnp.arange(nb)[:, None] * BLOCK + jnp.arange(BLOCK)[None, :]
    kpos = (jnp.arange(nb)[:, None] - 1) * BLOCK + jnp.arange(3 * BLOCK)[None, :]
    valid = ((jnp.abs(qpos[:, :, None] - kpos[:, None, :]) <= WINDOW)
             & (kpos >= 0)[:, None, :] & (kpos < S)[:, None, :])

    def blk(args):
        qn, kn, vn, mn = args
        s_ctx = jnp.einsum('bqkgd,bpkd->bkgqp', qn, ck).astype(jnp.float32) * scale
        s_loc = jnp.einsum('bqkgd,bskd->bkgqs', qn, kn).astype(jnp.float32) * scale
        s_loc = jnp.where(mn[None, None, None], s_loc, NEG)
        p = sink_softmax(jnp.concatenate([s_ctx, s_loc], axis=-1), sink).astype(vn.dtype)
        return (jnp.einsum('bkgqp,bpkd->bqkgd', p[..., :P], cv)
                + jnp.einsum('bkgqs,bskd->bqkgd', p[..., P:], vn))

    o = lax.map(blk, (qb, kw, vw, valid))
    return o.transpose(1, 0, 2, 3, 4, 5).reshape(B, S, Q_DIM)


def multiscale_pool(u, w_pool, pool_scale):
    B, S, _ = u.shape
    uf = u.astype(jnp.float32)
    cs = jnp.pad(lax.cumsum(uf, axis=1), ((0, 0), (1, 0), (0, 0)))
    t = jnp.arange(S)
    outs = []
    for g, w in enumerate(POOL_SIZES):
        lo = jnp.clip(t - w // 2, 0, S)
        hi = jnp.clip(t + (w - w // 2), 0, S)
        sl = slice(g * POOL_GROUP_DIM, (g + 1) * POOL_GROUP_DIM)
        csg = cs[:, :, sl]
        mean = (csg[:, hi] - csg[:, lo]) / (hi - lo).astype(jnp.float32)[None, :, None]
        outs.append(mean - uf[:, :, sl])
    pooled = jnp.stack(outs, axis=2).astype(u.dtype)
    mixed = jnp.einsum('bsgc,gcd->bsgd', pooled, w_pool).reshape(B, S, POOL_DIM)
    return mixed * pool_scale


def _split_in(proj):
    q = proj[..., :Q_DIM]
    k = proj[..., Q_DIM:Q_DIM + KV_DIM]
    v = proj[..., Q_DIM + KV_DIM:Q_DIM + 2 * KV_DIM]
    u = proj[..., Q_DIM + 2 * KV_DIM:]
    return q, k, v, u


def attn_pool_context(h, w_in, sink, w_pool, pool_scale, w_out):
    B, S, _ = h.shape
    q, k, v, u = _split_in(h @ w_in)
    q = q.reshape(B, S, N_HEADS, HEAD_DIM)
    k = k.reshape(B, S, N_KV_HEADS, HEAD_DIM)
    v = v.reshape(B, S, N_KV_HEADS, HEAD_DIM)
    att = context_attention(q, k, v, sink)
    pool = multiscale_pool(u, w_pool, pool_scale)
    out = jnp.concatenate([att, pool], axis=-1) @ w_out
    return out, k, v


def attn_pool_latent(h, ck, cv, w_in, sink, w_pool, pool_scale, w_out):
    B, S, _ = h.shape
    q, k, v, u = _split_in(h @ w_in)
    q = axial_rope(q.reshape(B, S, N_HEADS, HEAD_DIM))
    k = axial_rope(k.reshape(B, S, N_KV_HEADS, HEAD_DIM))
    v = v.reshape(B, S, N_KV_HEADS, HEAD_DIM)
    att = latent_attention(q, k, v, ck, cv, sink)
    pool = multiscale_pool(u, w_pool, pool_scale)
    return jnp.concatenate([att, pool], axis=-1) @ w_out


def conformer_conv(h, w1, b1, w_dw, b_dw, ln_g, ln_b, w2, b2):
    a = h @ w1 + b1
    a = a[..., :CONV_DIM] * jax.nn.sigmoid(a[..., CONV_DIM:])
    a = lax.conv_general_dilated(a, w_dw[:, None, :].astype(a.dtype), window_strides=(1,),
                                 padding=((CONV_WIDTH // 2, CONV_WIDTH // 2),),
                                 dimension_numbers=('NWC', 'WIO', 'NWC'),
                                 feature_group_count=CONV_DIM) + b_dw
    a = jax.nn.silu(layer_norm(a, ln_g, ln_b))
    return a @ w2 + b2


def peer(h, w_q, sub_keys, u_tab, v_tab):
    B, S, D = h.shape
    T = B * S
    x = h.reshape(T, D)
    q = (x @ w_q).reshape(T, PEER_HEADS, 2, PEER_HALF)
    s = jnp.einsum('thpc,hpnc->thpn', q, sub_keys).astype(jnp.float32)
    s1, i1 = lax.top_k(s[:, :, 0], PEER_TOPK)
    s2, i2 = lax.top_k(s[:, :, 1], PEER_TOPK)
    cand = (s1[..., :, None] + s2[..., None, :]).reshape(T, PEER_HEADS, PEER_TOPK * PEER_TOPK)
    cidx = (i1[..., :, None] * N_KEYS + i2[..., None, :]).reshape(T, PEER_HEADS, PEER_TOPK * PEER_TOPK)
    top_s, pos = lax.top_k(cand, PEER_TOPK)
    idx = jnp.take_along_axis(cidx, pos, axis=-1).reshape(T, PEER_HEADS * PEER_TOPK)
    g = jax.nn.softmax(top_s, axis=-1).reshape(T, PEER_HEADS * PEER_TOPK).astype(x.dtype)
    nc = T // PEER_CHUNK

    def chunk_fn(args):
        xc, ic, gc = args
        uc = jnp.take(u_tab, ic, axis=0)
        a = jnp.einsum('cd,ckd->ck', xc, uc)
        wgt = gc * jax.nn.gelu(a, approximate=False)
        vc = jnp.take(v_tab, ic, axis=0)
        return jnp.einsum('ck,ckd->cd', wgt, vc)

    y = lax.map(chunk_fn, (x.reshape(nc, PEER_CHUNK, D),
                           idx.reshape(nc, PEER_CHUNK, -1),
                           g.reshape(nc, PEER_CHUNK, -1)))
    return y.reshape(B, S, D)


def setup_inputs(seed: int = 0) -> dict:
    key = jax.random.key(seed)
    ks = iter(jax.random.split(key, 40))
    nrm = lambda shape, s=1.0: jax.random.normal(next(ks), shape, jnp.float32) * s
    D = D_MODEL
    L, LA, LC = DEPTH, N_ATTN_LAYERS, N_CONV_LAYERS
    return {
        'x_prompt': nrm((BATCH, SEQ, D)),
        'x_sample': nrm((DEC_BATCH, DEC_SEQ, D)),
        'c': nrm((DEC_BATCH, D)),
        'cache_k': nrm((DEC_BATCH, LA, PAST_LEN, N_KV_HEADS, HEAD_DIM)),
        'cache_v': nrm((DEC_BATCH, LA, PAST_LEN, N_KV_HEADS, HEAD_DIM)),
        'c_ctx': nrm((D,)),
        'mod_w': nrm((L, D, 6 * D), 0.5 * D ** -0.5),
        'mod_b': nrm((L, 6 * D), 0.02),
        'norm_mix_g': 1.0 + nrm((L, D), 0.1),
        'norm_ffn_g': 1.0 + nrm((L, D), 0.1),
        'w_in': nrm((LA, D, IN_DIM), D ** -0.5),
        'attn_sink': nrm((LA, N_HEADS)),
        'w_pool': nrm((LA, N_POOL_GROUPS, POOL_GROUP_DIM, POOL_GROUP_DIM), POOL_GROUP_DIM ** -0.5),
        'pool_scale': 1.0 + nrm((LA, POOL_DIM), 0.1),
        'w_out': nrm((LA, MIX_OUT, D), MIX_OUT ** -0.5),
        'conv_w1': nrm((LC, D, 2 * CONV_DIM), D ** -0.5),
        'conv_b1': nrm((LC, 2 * CONV_DIM), 0.02),
        'conv_dw': nrm((LC, CONV_WIDTH, CONV_DIM), CONV_WIDTH ** -0.5),
        'conv_dw_b': nrm((LC, CONV_DIM), 0.02),
        'conv_ln_g': 1.0 + nrm((LC, CONV_DIM), 0.1),
        'conv_ln_b': nrm((LC, CONV_DIM), 0.02),
        'conv_w2': nrm((LC, CONV_DIM, D), CONV_DIM ** -0.5),
        'conv_b2': nrm((LC, D), 0.02),
        'peer_wq': nrm((L, D, PEER_HEADS * PEER_QDIM), D ** -0.5),
        'peer_keys': nrm((L, PEER_HEADS, 2, N_KEYS, PEER_HALF), PEER_HALF ** -0.5),
        'peer_u': nrm((L, N_EXPERTS, D), D ** -0.5),
        'peer_v': nrm((L, N_EXPERTS, D), 1.0),
        'final_norm_g': 1.0 + nrm((D,), 0.1),
    }


def reference(x_prompt, x_sample, c, cache_k, cache_v, c_ctx, mod_w, mod_b, norm_mix_g, norm_ffn_g,
              w_in, attn_sink, w_pool, pool_scale, w_out, conv_w1, conv_b1, conv_dw, conv_dw_b,
              conv_ln_g, conv_ln_b, conv_w2, conv_b2, peer_wq, peer_keys, peer_u, peer_v, final_norm_g):
    xp = x_prompt
    xs = x_sample
    new_k = []
    new_v = []
    for l in range(DEPTH):
        j = l // 2
        mp = adaln(c_ctx[None, :], mod_w[l], mod_b[l])
        ms = adaln(c, mod_w[l], mod_b[l])
        hp = modulate(rms_norm(xp, norm_mix_g[l]), mp[0], mp[1])
        hs = modulate(rms_norm(xs, norm_mix_g[l]), ms[0], ms[1])
        if l % 2 == 0:
            op, kc, vc = attn_pool_context(hp, w_in[j], attn_sink[j], w_pool[j], pool_scale[j], w_out[j])
            os_ = attn_pool_latent(hs, cache_k[:, j], cache_v[:, j], w_in[j], attn_sink[j],
                                   w_pool[j], pool_scale[j], w_out[j])
            new_k.append(kc)
            new_v.append(vc)
        else:
            op = conformer_conv(hp, conv_w1[j], conv_b1[j], conv_dw[j], conv_dw_b[j],
                                conv_ln_g[j], conv_ln_b[j], conv_w2[j], conv_b2[j])
            os_ = conformer_conv(hs, conv_w1[j], conv_b1[j], conv_dw[j], conv_dw_b[j],
                                 conv_ln_g[j], conv_ln_b[j], conv_w2[j], conv_b2[j])
        xp = xp + mp[2] * op
        xs = xs + ms[2] * os_
        hp = modulate(rms_norm(xp, norm_ffn_g[l]), mp[3], mp[4])
        hs = modulate(rms_norm(xs, norm_ffn_g[l]), ms[3], ms[4])
        xp = xp + mp[5] * peer(hp, peer_wq[l], peer_keys[l], peer_u[l], peer_v[l])
        xs = xs + ms[5] * peer(hs, peer_wq[l], peer_keys[l], peer_u[l], peer_v[l])
    y_prompt = rms_norm(xp, final_norm_g)
    y_sample = rms_norm(xs, final_norm_g)
    state_k = jnp.stack(new_k, axis=1)
    state_v = jnp.stack(new_v, axis=1)
    return (y_prompt, y_sample, state_k, state_v)
```

```python
import functools
import math

import jax
import jax.numpy as jnp
from jax import lax
from jax.experimental import pallas as pl
from jax.experimental.pallas import tpu as pltpu

F32 = jnp.float32
BF16 = jnp.bfloat16

D = 1024
N_CTX_SEQ = 16
SEQ_CTX = 256
N_LAT_SEQ = 2
SEQ_LAT = 2048
T_CTX = N_CTX_SEQ * SEQ_CTX
T_LAT = N_LAT_SEQ * SEQ_LAT
T = T_CTX + T_LAT
N_GROUPS = 8
GRID_W = 64
ROPE_BASE = 10000.0
N_HEADS = 8
N_KV = 2
GROUPS = N_HEADS // N_KV
HD = 64
WINDOW = 128
Q_DIM = N_HEADS * HD
KV_DIM = N_KV * HD
POOL_SIZES = (2, 4, 8, 16)
POOL_DIM = 512
PGD = 128
IN_DIM = Q_DIM + 2 * KV_DIM + POOL_DIM
CONV_W = 31
N_KEYS = 128
N_EXPERTS = N_KEYS * N_KEYS
P_HEADS = 8
P_TOPK = 16
EPS = 1e-6
NEG = -1e30
INF = float("inf")

VMEM_LIMIT = 56 * 1024 * 1024


def _cparams(sem):
    return pltpu.CompilerParams(dimension_semantics=sem, vmem_limit_bytes=VMEM_LIMIT)


def _group_of_tile(i, tile):
    n_ctx = T_CTX // tile
    per_seq = SEQ_LAT // tile
    return jnp.where(i < n_ctx, 0, 1 + (i - n_ctx) // per_seq)


def _modnorm(x, g, shift, scale):
    y = x * lax.rsqrt(jnp.mean(x * x, axis=-1, keepdims=True) + EPS)
    y = y * g
    return y * (1 + scale) + shift


def _adaln_kernel(cv_ref, w_ref, b_ref, o_ref):
    a = jax.nn.silu(cv_ref[...]).astype(BF16)
    o_ref[...] = jnp.dot(a, w_ref[...].astype(BF16), preferred_element_type=F32) + b_ref[...]


def _adaln(cvec, mod_w, mod_b):
    L, _, n6 = mod_w.shape
    tn = 1536
    return pl.pallas_call(
        _adaln_kernel,
        grid=(L, n6 // tn),
        in_specs=[
            pl.BlockSpec((N_GROUPS, D), lambda l, n: (0, 0)),
            pl.BlockSpec((None, D, tn), lambda l, n: (l, 0, n)),
            pl.BlockSpec((None, 1, tn), lambda l, n: (l, 0, n)),
        ],
        out_specs=pl.BlockSpec((None, N_GROUPS, tn), lambda l, n: (l, 0, n)),
        out_shape=jax.ShapeDtypeStruct((L, N_GROUPS, n6), F32),
        compiler_params=_cparams(("parallel", "parallel")),
        name="adaln",
    )(cvec, mod_w, mod_b.reshape(L, 1, n6))


def _inproj_kernel(x_ref, g_ref, mod_ref, w_ref, q_ref, k_ref, v_ref, u_ref):
    h = _modnorm(x_ref[...], g_ref[...], mod_ref[0:1, :], mod_ref[1:2, :])
    p = jnp.dot(h.astype(BF16), w_ref[...], preferred_element_type=F32)
    q_ref[...] = p[:, :Q_DIM]
    k_ref[...] = p[:, Q_DIM:Q_DIM + KV_DIM]
    v_ref[...] = p[:, Q_DIM + KV_DIM:Q_DIM + 2 * KV_DIM]
    u_ref[...] = p[:, Q_DIM + 2 * KV_DIM:]


def _inproj(x, g, mods, w_in_b):
    tt = 512
    return pl.pallas_call(
        _inproj_kernel,
        grid=(T // tt,),
        in_specs=[
            pl.BlockSpec((tt, D), lambda i: (i, 0)),
            pl.BlockSpec((1, D), lambda i: (0, 0)),
            pl.BlockSpec((None, 6, D), lambda i: (_group_of_tile(i, tt), 0, 0)),
            pl.BlockSpec((D, IN_DIM), lambda i: (0, 0)),
        ],
        out_specs=[
            pl.BlockSpec((tt, Q_DIM), lambda i: (i, 0)),
            pl.BlockSpec((tt, KV_DIM), lambda i: (i, 0)),
            pl.BlockSpec((tt, KV_DIM), lambda i: (i, 0)),
            pl.BlockSpec((tt, POOL_DIM), lambda i: (i, 0)),
        ],
        out_shape=[
            jax.ShapeDtypeStruct((T, Q_DIM), F32),
            jax.ShapeDtypeStruct((T, KV_DIM), F32),
            jax.ShapeDtypeStruct((T, KV_DIM), F32),
            jax.ShapeDtypeStruct((T, POOL_DIM), F32),
        ],
        compiler_params=_cparams(("parallel",)),
        name="inproj",
    )(x, g, mods, w_in_b)


def _softmax_parts(parts, sk):
    m = sk
    for s in parts:
        m = jnp.maximum(m, jnp.max(s, axis=-1, keepdims=True))
    es = [jnp.exp(s - m) for s in parts]
    den = jnp.exp(sk - m)
    for e in es:
        den = den + jnp.sum(e, axis=-1, keepdims=True)
    return [(e / den).astype(BF16) for e in es]


_NT = (((1,), (1,)), ((), ()))


def _ctx_attn_kernel(sink_ref, q_ref, k_ref, v_ref, o_ref):
    scale = HD ** -0.5
    for j in range(N_KV):
        kj = k_ref[:, j * HD:(j + 1) * HD].astype(BF16)
        vj = v_ref[:, j * HD:(j + 1) * HD].astype(BF16)
        for g in range(GROUPS):
            hd = j * GROUPS + g
            qh = q_ref[:, hd * HD:(hd + 1) * HD].astype(BF16)
            s = lax.dot_general(qh, kj, _NT, preferred_element_type=F32) * scale
            (p,) = _softmax_parts([s], sink_ref[hd])
            o_ref[:, hd * HD:(hd + 1) * HD] = jnp.dot(p, vj, preferred_element_type=F32)


def _ctx_attn(sink, q, k, v):
    return pl.pallas_call(
        _ctx_attn_kernel,
        grid=(N_CTX_SEQ,),
        in_specs=[
            pl.BlockSpec(memory_space=pltpu.SMEM),
            pl.BlockSpec((SEQ_CTX, Q_DIM), lambda b: (b, 0)),
            pl.BlockSpec((SEQ_CTX, KV_DIM), lambda b: (b, 0)),
            pl.BlockSpec((SEQ_CTX, KV_DIM), lambda b: (b, 0)),
        ],
        out_specs=pl.BlockSpec((SEQ_CTX, Q_DIM), lambda b: (b, 0)),
        out_shape=jax.ShapeDtypeStruct((T_CTX, Q_DIM), F32),
        compiler_params=_cparams(("parallel",)),
        name="ctx_attn",
    )(sink, q, k, v)


def _rope(x, cos, sin_signed):
    n = x.shape[-1]
    lane = lax.broadcasted_iota(jnp.int32, x.shape, 1)
    up = pltpu.roll(x, n - 16, 1)
    dn = pltpu.roll(x, 16, 1)
    partner = jnp.where((lane & 16) == 0, up, dn)
    return x * cos + partner * sin_signed


QB = 128
KWIN = 3 * QB


def _lat_attn_kernel(sink_ref, q_ref, k_ref, v_ref, ck_ref, cv_ref, cq_ref, sq_ref, ckk_ref, skk_ref, o_ref):
    scale = HD ** -0.5
    qb = pl.program_id(1)
    start = pl.multiple_of(jnp.clip((qb - 1) * QB, 0, SEQ_LAT - KWIN), QB)
    q = _rope(q_ref[...], cq_ref[...], sq_ref[...])
    kw = _rope(k_ref[pl.ds(start, KWIN), :], ckk_ref[pl.ds(start, KWIN), :], skk_ref[pl.ds(start, KWIN), :])
    vw = v_ref[pl.ds(start, KWIN), :]
    qpos = qb * QB + lax.broadcasted_iota(jnp.int32, (QB, KWIN), 0)
    kpos = start + lax.broadcasted_iota(jnp.int32, (QB, KWIN), 1)
    valid = jnp.abs(qpos - kpos) <= WINDOW
    for j in range(N_KV):
        sl = slice(j * HD, (j + 1) * HD)
        kj = kw[:, sl].astype(BF16)
        vj = vw[:, sl].astype(BF16)
        ckj = ck_ref[:, sl].astype(BF16)
        cvj = cv_ref[:, sl].astype(BF16)
        for g in range(GROUPS):
            hd = j * GROUPS + g
            qh = q[:, hd * HD:(hd + 1) * HD].astype(BF16)
            s_ctx = lax.dot_general(qh, ckj, _NT, preferred_element_type=F32) * scale
            s_loc = lax.dot_general(qh, kj, _NT, preferred_element_type=F32) * scale
            s_loc = jnp.where(valid, s_loc, NEG)
            p_ctx, p_loc = _softmax_parts([s_ctx, s_loc], sink_ref[hd])
            o_ref[:, hd * HD:(hd + 1) * HD] = (jnp.dot(p_ctx, cvj, preferred_element_type=F32)
                                               + jnp.dot(p_loc, vj, preferred_element_type=F32))


def _rope_tables():
    pos = jnp.arange(SEQ_LAT)
    rows = (pos // GRID_W).astype(F32)
    cols = (pos % GRID_W).astype(F32)
    quarter = HD // 4
    freqs = ROPE_BASE ** (-jnp.arange(quarter, dtype=F32) / quarter)
    ar = rows[:, None] * freqs[None, :]
    ac = cols[:, None] * freqs[None, :]
    cos = jnp.concatenate([jnp.cos(ar), jnp.cos(ar), jnp.cos(ac), jnp.cos(ac)], axis=-1)
    sin = jnp.concatenate([-jnp.sin(ar), jnp.sin(ar), -jnp.sin(ac), jnp.sin(ac)], axis=-1)
    return cos, sin


def _lat_attn(sink, q, k, v, ck, cv):
    cos, sin = _rope_tables()
    cq, sq = jnp.tile(cos, (1, N_HEADS)), jnp.tile(sin, (1, N_HEADS))
    ckk, skk = jnp.tile(cos, (1, N_KV)), jnp.tile(sin, (1, N_KV))
    full = lambda shape: pl.BlockSpec(shape, lambda b, i: (0, 0))
    per_b = lambda rows, cols: pl.BlockSpec((None, rows, cols), lambda b, i: (b, 0, 0))
    out = pl.pallas_call(
        _lat_attn_kernel,
        grid=(N_LAT_SEQ, SEQ_LAT // QB),
        in_specs=[
            pl.BlockSpec(memory_space=pltpu.SMEM),
            pl.BlockSpec((None, QB, Q_DIM), lambda b, i: (b, i, 0)),
            per_b(SEQ_LAT, KV_DIM),
            per_b(SEQ_LAT, KV_DIM),
            per_b(ck.shape[1], KV_DIM),
            per_b(cv.shape[1], KV_DIM),
            pl.BlockSpec((QB, Q_DIM), lambda b, i: (i, 0)),
            pl.BlockSpec((QB, Q_DIM), lambda b, i: (i, 0)),
            full((SEQ_LAT, KV_DIM)),
            full((SEQ_LAT, KV_DIM)),
        ],
        out_specs=pl.BlockSpec((None, QB, Q_DIM), lambda b, i: (b, i, 0)),
        out_shape=jax.ShapeDtypeStruct((N_LAT_SEQ, SEQ_LAT, Q_DIM), F32),
        compiler_params=_cparams(("parallel", "parallel")),
        name="lat_attn",
    )(sink, q.reshape(N_LAT_SEQ, SEQ_LAT, Q_DIM), k.reshape(N_LAT_SEQ, SEQ_LAT, KV_DIM),
      v.reshape(N_LAT_SEQ, SEQ_LAT, KV_DIM), ck, cv, cq, sq, ckk, skk)
    return out.reshape(T_LAT, Q_DIM)


ST = 256
N_ST = T // ST


def _seq_tile_info(i):
    n_ctx = T_CTX // ST
    per_seq = SEQ_LAT // ST
    is_ctx = i < n_ctx
    pos0 = jnp.where(is_ctx, 0, ((i - n_ctx) % per_seq) * ST)
    slen = jnp.where(is_ctx, SEQ_CTX, SEQ_LAT)
    return pos0, slen


def _halo_specs(cols):
    return [
        pl.BlockSpec((ST, cols), lambda i: (jnp.maximum(i - 1, 0), 0)),
        pl.BlockSpec((ST, cols), lambda i: (i, 0)),
        pl.BlockSpec((ST, cols), lambda i: (jnp.minimum(i + 1, N_ST - 1), 0)),
    ]


def _with_halo(prev_ref, cur_ref, next_ref, halo, pos0, slen):
    has_prev = (pos0 > 0).astype(F32)
    has_next = (pos0 + ST < slen).astype(F32)
    return jnp.concatenate([prev_ref[ST - halo:, :] * has_prev, cur_ref[...], next_ref[:halo, :] * has_next], axis=0)


def _poolout_kernel(att_ref, up_ref, uc_ref, un_ref, x_ref, mod_ref, wp_ref, ps_ref, wo_ref, o_ref):
    i = pl.program_id(0)
    pos0, slen = _seq_tile_info(i)
    halo = 8
    n = ST + 2 * halo
    ext = _with_halo(up_ref, uc_ref, un_ref, halo, pos0, slen)
    t = pos0 + lax.broadcasted_iota(jnp.int32, (ST, 1), 0)
    out = jnp.dot(att_ref[...].astype(BF16), wo_ref[:Q_DIM, :], preferred_element_type=F32)
    for g, w in enumerate(POOL_SIZES):
        sl = slice(g * PGD, (g + 1) * PGD)
        e = ext[:, sl]
        s = e + pltpu.roll(e, 1, 0)
        step = 1
        while 2 * step < w:
            s = pltpu.roll(s, step, 0) + pltpu.roll(s, n - step, 0)
            step *= 2
        lo = jnp.maximum(t - w // 2, 0)
        hi = jnp.minimum(t + (w - w // 2), slen)
        mean = s[halo:halo + ST, :] / (hi - lo).astype(F32)
        pooled = mean - e[halo:halo + ST, :]
        mixed = jnp.dot(pooled.astype(BF16), wp_ref[g], preferred_element_type=F32) * ps_ref[:, sl]
        out = out + jnp.dot(mixed.astype(BF16), wo_ref[Q_DIM + g * PGD:Q_DIM + (g + 1) * PGD, :],
                            preferred_element_type=F32)
    o_ref[...] = x_ref[...] + mod_ref[2:3, :] * out


def _poolout(att, u, x, mods, w_pool_b, pool_scale, w_out_b):
    return pl.pallas_call(
        _poolout_kernel,
        grid=(N_ST,),
        in_specs=[pl.BlockSpec((ST, Q_DIM), lambda i: (i, 0))] + _halo_specs(POOL_DIM) + [
            pl.BlockSpec((ST, D), lambda i: (i, 0)),
            pl.BlockSpec((None, 6, D), lambda i: (_group_of_tile(i, ST), 0, 0)),
            pl.BlockSpec((len(POOL_SIZES), PGD, PGD), lambda i: (0, 0, 0)),
            pl.BlockSpec((1, POOL_DIM), lambda i: (0, 0)),
            pl.BlockSpec((Q_DIM + POOL_DIM, D), lambda i: (0, 0)),
        ],
        out_specs=pl.BlockSpec((ST, D), lambda i: (i, 0)),
        out_shape=jax.ShapeDtypeStruct((T, D), F32),
        compiler_params=_cparams(("parallel",)),
        name="poolout",
    )(att, u, u, u, x, mods, w_pool_b, pool_scale, w_out_b)


def _glu_kernel(x_ref, g_ref, mod_ref, w_ref, b_ref, o_ref):
    h = _modnorm(x_ref[...], g_ref[...], mod_ref[0:1, :], mod_ref[1:2, :])
    a = jnp.dot(h.astype(BF16), w_ref[...], preferred_element_type=F32) + b_ref[...]
    o_ref[...] = a[:, :D] * jax.nn.sigmoid(a[:, D:])


def _glu(x, g, mods, w1_b, b1):
    tt = 512
    return pl.pallas_call(
        _glu_kernel,
        grid=(T // tt,),
        in_specs=[
            pl.BlockSpec((tt, D), lambda i: (i, 0)),
            pl.BlockSpec((1, D), lambda i: (0, 0)),
            pl.BlockSpec((None, 6, D), lambda i: (_group_of_tile(i, tt), 0, 0)),
            pl.BlockSpec((D, 2 * D), lambda i: (0, 0)),
            pl.BlockSpec((1, 2 * D), lambda i: (0, 0)),
        ],
        out_specs=pl.BlockSpec((tt, D), lambda i: (i, 0)),
        out_shape=jax.ShapeDtypeStruct((T, D), F32),
        compiler_params=_cparams(("parallel",)),
        name="glu",
    )(x, g, mods, w1_b, b1)


CONV_HALO = 16


def _convout_kernel(ap_ref, ac_ref, an_ref, x_ref, mod_ref, dw_ref, dwb_ref, lg_ref, lb_ref, w2_ref, b2_ref,
                    o_ref, ext_ref):
    i = pl.program_id(0)
    pos0, slen = _seq_tile_info(i)
    ext_ref[...] = _with_halo(ap_ref, ac_ref, an_ref, CONV_HALO, pos0, slen)
    pad = CONV_W // 2
    acc = jnp.zeros((ST, D), F32)
    for k in range(CONV_W):
        acc = acc + ext_ref[pl.ds(CONV_HALO - pad + k, ST), :] * dw_ref[k:k + 1, :]
    a = acc + dwb_ref[...]
    mu = jnp.mean(a, axis=-1, keepdims=True)
    var = jnp.mean(jnp.square(a - mu), axis=-1, keepdims=True)
    y = (a - mu) * lax.rsqrt(var + EPS) * lg_ref[...] + lb_ref[...]
    y = jax.nn.silu(y)
    out = jnp.dot(y.astype(BF16), w2_ref[...], preferred_element_type=F32) + b2_ref[...]
    o_ref[...] = x_ref[...] + mod_ref[2:3, :] * out


def _convout(a, x, mods, dw, dwb, lg, lb, w2_b, b2):
    row = lambda: pl.BlockSpec((1, D), lambda i: (0, 0))
    return pl.pallas_call(
        _convout_kernel,
        grid=(N_ST,),
        in_specs=_halo_specs(D) + [
            pl.BlockSpec((ST, D), lambda i: (i, 0)),
            pl.BlockSpec((None, 6, D), lambda i: (_group_of_tile(i, ST), 0, 0)),
            pl.BlockSpec((CONV_W, D), lambda i: (0, 0)),
            row(), row(), row(),
            pl.BlockSpec((D, D), lambda i: (0, 0)),
            row(),
        ],
        out_specs=pl.BlockSpec((ST, D), lambda i: (i, 0)),
        out_shape=jax.ShapeDtypeStruct((T, D), F32),
        scratch_shapes=[pltpu.VMEM((ST + 2 * CONV_HALO, D), F32)],
        compiler_params=_cparams(("parallel",)),
        name="convout",
    )(a, a, a, x, mods, dw, dwb, lg, lb, w2_b, b2)


SEL_TT = 256
LANES = 128


def _topk16(s):
    rows = s.shape[0]
    iota = lax.broadcasted_iota(jnp.int32, s.shape, 0).astype(F32)
    rank = jnp.full(s.shape, float(P_TOPK), F32)
    vals = []
    for r in range(P_TOPK):
        m = jnp.max(s, axis=0, keepdims=True)
        first = jnp.min(jnp.where(s == m, iota, float(rows)), axis=0, keepdims=True)
        hit = iota == first
        rank = jnp.where(hit, float(r), rank)
        s = jnp.where(hit, -INF, s)
        vals.append(m)
    return rank, jnp.concatenate(vals, axis=0)


_CAND_BLOCKS = [(0, 0, 8), (0, 8, 8)] + [(r1, 0, P_TOPK // (r1 + 1)) for r1 in range(1, 8)] + [(None, 0, 8)]


def _select_kernel(x_ref, g_ref, mod_ref, wq_ref, keys_ref, hb_ref, r_ref, e2_ref, c_ref, e1_ref, q_scr, s_scr):
    h = _modnorm(x_ref[...], g_ref[...], mod_ref[3:4, :], mod_ref[4:5, :])
    hb = h.astype(BF16)
    hb_ref[...] = hb
    q_scr[...] = jnp.dot(hb, wq_ref[...], preferred_element_type=F32).astype(BF16)
    iota8 = lax.broadcasted_iota(jnp.int32, (8, LANES), 0)

    def head_body(hd, carry):
        for p in range(2):
            col = pl.multiple_of((hd * 2 + p) * N_KEYS, N_KEYS)
            s_scr[p] = lax.dot_general(keys_ref[hd * 2 + p], q_scr[:, pl.ds(col, N_KEYS)], _NT,
                                       preferred_element_type=F32)
        for lc in range(SEL_TT // LANES):
            ls = slice(lc * LANES, (lc + 1) * LANES)
            s1 = s_scr[0, :, ls]
            s2 = s_scr[1, :, ls]
            rank1, v1 = _topk16(s1)
            rank2, v2 = _topk16(s2)
            blocks = []
            for r1, r2, nvalid in _CAND_BLOCKS:
                if r1 is None:
                    b = v1[8:16, :] + v2[0:1, :]
                else:
                    b = v1[r1:r1 + 1, :] + v2[r2:r2 + 8, :]
                if nvalid < 8:
                    b = jnp.where(iota8 < nvalid, b, -INF)
                blocks.append(b)
            cand = jnp.concatenate(blocks, axis=0)
            rank_c, _ = _topk16(cand)
            sel = rank_c < float(P_TOPK)
            self32 = sel.astype(F32)
            top = v1[0:1, :] + v2[0:1, :]
            z = jnp.sum(jnp.where(sel, jnp.exp(cand - top), 0.0), axis=0, keepdims=True)
            cnt_rows = [jnp.sum(self32[0:16, :], axis=0, keepdims=True)]
            for b in range(2, 9):
                cnt_rows.append(jnp.sum(self32[8 * b:8 * b + 8, :], axis=0, keepdims=True))
            cnt = jnp.concatenate(cnt_rows + [self32[72:80, :]], axis=0)
            c = jnp.zeros_like(s1)
            for r in range(P_TOPK):
                c = jnp.where(rank1 == float(r), cnt[r:r + 1, :], c)
            r_ref[hd, :, ls] = rank2.astype(BF16)
            e2_ref[hd, :, ls] = jnp.exp(s2 - v2[0:1, :]).astype(BF16)
            c_ref[hd, :, ls] = c
            e1_ref[hd, :, ls] = jnp.exp(s1 - v1[0:1, :]) * (1.0 / z)
        return carry

    lax.fori_loop(0, P_HEADS, head_body, 0)


def _select(x, g, mods, wq_b, keys_b):
    tt = SEL_TT
    hk = pl.BlockSpec((P_HEADS, N_KEYS, tt), lambda i: (0, 0, i))
    return pl.pallas_call(
        _select_kernel,
        grid=(T // tt,),
        in_specs=[
            pl.BlockSpec((tt, D), lambda i: (i, 0)),
            pl.BlockSpec((1, D), lambda i: (0, 0)),
            pl.BlockSpec((None, 6, D), lambda i: (_group_of_tile(i, tt), 0, 0)),
            pl.BlockSpec((D, 2 * P_HEADS * N_KEYS), lambda i: (0, 0)),
            pl.BlockSpec((2 * P_HEADS, N_KEYS, N_KEYS), lambda i: (0, 0, 0)),
        ],
        out_specs=[pl.BlockSpec((tt, D), lambda i: (i, 0)), hk, hk, hk, hk],
        out_shape=[
            jax.ShapeDtypeStruct((T, D), BF16),
            jax.ShapeDtypeStruct((P_HEADS, N_KEYS, T), BF16),
            jax.ShapeDtypeStruct((P_HEADS, N_KEYS, T), BF16),
            jax.ShapeDtypeStruct((P_HEADS, N_KEYS, T), F32),
            jax.ShapeDtypeStruct((P_HEADS, N_KEYS, T), F32),
        ],
        scratch_shapes=[pltpu.VMEM((tt, 2 * P_HEADS * N_KEYS), BF16), pltpu.VMEM((2, N_KEYS, tt), F32)],
        compiler_params=_cparams(("parallel",)),
        name="peer_select",
    )(x, g, mods, wq_b, keys_b)


DT = 512
DE = 1024
I1_PER = DE // N_KEYS
PACK = 16


def _dense_kernel(final_norm, hb_ref, u_ref, vt_ref, r_ref, e2_ref, c_ref, e1_ref, x_ref, mod_ref, fg_ref,
                  o_ref, acc_ref, p_ref):
    k = pl.program_id(1)

    @pl.when(k == 0)
    def _():
        acc_ref[...] = jnp.zeros_like(acc_ref)

    a_t = lax.dot_general(u_ref[...], hb_ref[...], _NT, preferred_element_type=F32)
    act = 0.5 * a_t * (1.0 + lax.erf(a_t * math.sqrt(0.5)))
    for ii in range(I1_PER):
        w = jnp.zeros((N_KEYS // PACK, PACK, DT), BF16)
        for hd in range(P_HEADS):
            cb = jnp.broadcast_to(c_ref[hd, ii:ii + 1, :], (PACK, DT)).astype(BF16)
            eb = jnp.broadcast_to(e1_ref[hd, ii:ii + 1, :], (PACK, DT)).astype(BF16)
            w = w + jnp.where(r_ref[hd] < cb[None], e2_ref[hd], jnp.zeros((), BF16)) * eb[None]
        rows = slice(ii * N_KEYS, (ii + 1) * N_KEYS)
        p_ref[rows, :] = (act[rows, :] * w.reshape(N_KEYS, DT).astype(F32)).astype(BF16)
    acc_ref[...] += jnp.dot(vt_ref[...], p_ref[...], preferred_element_type=F32)

    @pl.when(k == pl.num_programs(1) - 1)
    def _():
        y = x_ref[...] + mod_ref[5:6, :] * acc_ref[...].T
        if final_norm:
            y = y * lax.rsqrt(jnp.mean(y * y, axis=-1, keepdims=True) + EPS) * fg_ref[...]
        o_ref[...] = y


def _dense(hb, u_b, vt_b, rk, e2, c, e1, x, mods, fg, final_norm):
    hk_b = pl.BlockSpec((P_HEADS, N_KEYS // PACK, PACK, DT), lambda j, k: (0, 0, 0, j))
    hk_f = pl.BlockSpec((P_HEADS, I1_PER, DT), lambda j, k: (0, k, j))
    rk4 = rk.reshape(P_HEADS, N_KEYS // PACK, PACK, T)
    e24 = e2.reshape(P_HEADS, N_KEYS // PACK, PACK, T)
    return pl.pallas_call(
        functools.partial(_dense_kernel, final_norm),
        grid=(T // DT, N_EXPERTS // DE),
        in_specs=[
            pl.BlockSpec((DT, D), lambda j, k: (j, 0)),
            pl.BlockSpec((DE, D), lambda j, k: (k, 0)),
            pl.BlockSpec((D, DE), lambda j, k: (0, k)),
            hk_b, hk_b, hk_f, hk_f,
            pl.BlockSpec((DT, D), lambda j, k: (j, 0)),
            pl.BlockSpec((None, 6, D), lambda j, k: (_group_of_tile(j, DT), 0, 0)),
            pl.BlockSpec((1, D), lambda j, k: (0, 0)),
        ],
        out_specs=pl.BlockSpec((DT, D), lambda j, k: (j, 0)),
        out_shape=jax.ShapeDtypeStruct((T, D), F32),
        scratch_shapes=[pltpu.VMEM((D, DT), F32), pltpu.VMEM((DE, DT), BF16)],
        compiler_params=_cparams(("parallel", "arbitrary")),
        name="peer_dense",
    )(hb, u_b, vt_b, rk4, e24, c, e1, x, mods, fg)


def _peer(x, g, mods, wq, keys, u_tab, v_tab, fg, final_norm):
    hb, rk, e2, c, e1 = _select(x, g, mods, wq.astype(BF16),
                                keys.reshape(2 * P_HEADS, N_KEYS, N_KEYS).astype(BF16))
    return _dense(hb, u_tab.astype(BF16), v_tab.T.astype(BF16), rk, e2, c, e1, x, mods, fg, final_norm)


def kernel(x_prompt, x_sample, c, cache_k, cache_v, c_ctx, mod_w, mod_b, norm_mix_g, norm_ffn_g, w_in, attn_sink, w_pool, pool_scale, w_out, conv_w1, conv_b1, conv_dw, conv_dw_b, conv_ln_g, conv_ln_b, conv_w2, conv_b2, peer_wq, peer_keys, peer_u, peer_v, final_norm_g):
    x = jnp.concatenate([x_prompt.reshape(T_CTX, D), x_sample.reshape(T_LAT, D)], axis=0)
    cvec = jnp.concatenate([c_ctx[None, :], c, jnp.zeros((N_GROUPS - 1 - N_LAT_SEQ, D), F32)], axis=0)
    mods_all = _adaln(cvec, mod_w, mod_b).reshape(mod_w.shape[0], N_GROUPS, 6, D)
    row = lambda a: a.reshape(1, -1)
    fg = row(final_norm_g)

    mods = mods_all[0]
    q, k, v, u = _inproj(x, row(norm_mix_g[0]), mods, w_in[0].astype(BF16))
    att_ctx = _ctx_attn(attn_sink[0], q[:T_CTX], k[:T_CTX], v[:T_CTX])
    n_past = cache_k.shape[2]
    att_lat = _lat_attn(attn_sink[0], q[T_CTX:], k[T_CTX:], v[T_CTX:],
                        cache_k[:, 0].reshape(N_LAT_SEQ, n_past, KV_DIM),
                        cache_v[:, 0].reshape(N_LAT_SEQ, n_past, KV_DIM))
    att = jnp.concatenate([att_ctx, att_lat], axis=0)
    x = _poolout(att, u, x, mods, w_pool[0].astype(BF16), row(pool_scale[0]), w_out[0].astype(BF16))
    x = _peer(x, row(norm_ffn_g[0]), mods, peer_wq[0], peer_keys[0], peer_u[0], peer_v[0], fg, False)
    state_k = k[:T_CTX].reshape(N_CTX_SEQ, 1, SEQ_CTX, N_KV, HD)
    state_v = v[:T_CTX].reshape(N_CTX_SEQ, 1, SEQ_CTX, N_KV, HD)

    mods = mods_all[1]
    a = _glu(x, row(norm_mix_g[1]), mods, conv_w1[0].astype(BF16), row(conv_b1[0]))
    x = _convout(a, x, mods, conv_dw[0], row(conv_dw_b[0]), row(conv_ln_g[0]), row(conv_ln_b[0]),
                 conv_w2[0].astype(BF16), row(conv_b2[0]))
    x = _peer(x, row(norm_ffn_g[1]), mods, peer_wq[1], peer_keys[1], peer_u[1], peer_v[1], fg, True)

    y_prompt = x[:T_CTX].reshape(N_CTX_SEQ, SEQ_CTX, D)
    y_sample = x[T_CTX:].reshape(N_LAT_SEQ, SEQ_LAT, D)
    return (y_prompt, y_sample, state_k, state_v)
```

```python
import functools
import math

import jax
import jax.numpy as jnp
from jax import lax
from jax.experimental import pallas as pl
from jax.experimental.pallas import tpu as pltpu

F32 = jnp.float32
BF16 = jnp.bfloat16

D = 1024
N_CTX_SEQ = 16
SEQ_CTX = 256
N_LAT_SEQ = 2
SEQ_LAT = 2048
T_CTX = N_CTX_SEQ * SEQ_CTX
T_LAT = N_LAT_SEQ * SEQ_LAT
T = T_CTX + T_LAT
N_GROUPS = 8
GRID_W = 64
ROPE_BASE = 10000.0
N_HEADS = 8
N_KV = 2
GROUPS = N_HEADS // N_KV
HD = 64
WINDOW = 128
Q_DIM = N_HEADS * HD
KV_DIM = N_KV * HD
POOL_SIZES = (2, 4, 8, 16)
POOL_DIM = 512
PGD = 128
IN_DIM = Q_DIM + 2 * KV_DIM + POOL_DIM
CONV_W = 31
N_KEYS = 128
N_EXPERTS = N_KEYS * N_KEYS
P_HEADS = 8
P_TOPK = 16
EPS = 1e-6
NEG = -1e30
INF = float("inf")

VMEM_LIMIT = 56 * 1024 * 1024


def _cparams(sem, flags=None):
    return pltpu.CompilerParams(dimension_semantics=sem, vmem_limit_bytes=VMEM_LIMIT, flags=flags)


def _group_of_tile(i, tile):
    n_ctx = T_CTX // tile
    per_seq = SEQ_LAT // tile
    return jnp.where(i < n_ctx, 0, 1 + (i - n_ctx) // per_seq)


def _modnorm(x, g, shift, scale):
    y = x * lax.rsqrt(jnp.mean(x * x, axis=-1, keepdims=True) + EPS)
    y = y * g
    return y * (1 + scale) + shift


def _adaln_kernel(cv_ref, w_ref, b_ref, o_ref):
    a = jax.nn.silu(cv_ref[...]).astype(BF16)
    o_ref[...] = jnp.dot(a, w_ref[...].astype(BF16), preferred_element_type=F32) + b_ref[...]


def _adaln(cvec, mod_w, mod_b):
    L, _, n6 = mod_w.shape
    tn = 1536
    return pl.pallas_call(
        _adaln_kernel,
        grid=(L, n6 // tn),
        in_specs=[
            pl.BlockSpec((N_GROUPS, D), lambda l, n: (0, 0)),
            pl.BlockSpec((None, D, tn), lambda l, n: (l, 0, n)),
            pl.BlockSpec((None, 1, tn), lambda l, n: (l, 0, n)),
        ],
        out_specs=pl.BlockSpec((None, N_GROUPS, tn), lambda l, n: (l, 0, n)),
        out_shape=jax.ShapeDtypeStruct((L, N_GROUPS, n6), F32),
        compiler_params=_cparams(("parallel", "parallel")),
        name="adaln",
    )(cvec, mod_w, mod_b.reshape(L, 1, n6))


def _inproj_kernel(x_ref, g_ref, mod_ref, w_ref, q_ref, k_ref, v_ref, u_ref):
    h = _modnorm(x_ref[...], g_ref[...], mod_ref[0:1, :], mod_ref[1:2, :])
    p = jnp.dot(h.astype(BF16), w_ref[...], preferred_element_type=F32)
    q_ref[...] = p[:, :Q_DIM]
    k_ref[...] = p[:, Q_DIM:Q_DIM + KV_DIM]
    v_ref[...] = p[:, Q_DIM + KV_DIM:Q_DIM + 2 * KV_DIM]
    u_ref[...] = p[:, Q_DIM + 2 * KV_DIM:]


def _inproj(x, g, mods, w_in_b):
    tt = 512
    return pl.pallas_call(
        _inproj_kernel,
        grid=(T // tt,),
        in_specs=[
            pl.BlockSpec((tt, D), lambda i: (i, 0)),
            pl.BlockSpec((1, D), lambda i: (0, 0)),
            pl.BlockSpec((None, 6, D), lambda i: (_group_of_tile(i, tt), 0, 0)),
            pl.BlockSpec((D, IN_DIM), lambda i: (0, 0)),
        ],
        out_specs=[
            pl.BlockSpec((tt, Q_DIM), lambda i: (i, 0)),
            pl.BlockSpec((tt, KV_DIM), lambda i: (i, 0)),
            pl.BlockSpec((tt, KV_DIM), lambda i: (i, 0)),
            pl.BlockSpec((tt, POOL_DIM), lambda i: (i, 0)),
        ],
        out_shape=[
            jax.ShapeDtypeStruct((T, Q_DIM), F32),
            jax.ShapeDtypeStruct((T, KV_DIM), F32),
            jax.ShapeDtypeStruct((T, KV_DIM), F32),
            jax.ShapeDtypeStruct((T, POOL_DIM), F32),
        ],
        compiler_params=_cparams(("parallel",)),
        name="inproj",
    )(x, g, mods, w_in_b)


def _softmax_parts(parts, sk):
    m = sk
    for s in parts:
        m = jnp.maximum(m, jnp.max(s, axis=-1, keepdims=True))
    es = [jnp.exp(s - m) for s in parts]
    den = jnp.exp(sk - m)
    for e in es:
        den = den + jnp.sum(e, axis=-1, keepdims=True)
    return [(e / den).astype(BF16) for e in es]


_NT = (((1,), (1,)), ((), ()))


def _ctx_attn_kernel(sink_ref, q_ref, k_ref, v_ref, o_ref):
    scale = HD ** -0.5
    for j in range(N_KV):
        kj = k_ref[:, j * HD:(j + 1) * HD].astype(BF16)
        vj = v_ref[:, j * HD:(j + 1) * HD].astype(BF16)
        for g in range(GROUPS):
            hd = j * GROUPS + g
            qh = q_ref[:, hd * HD:(hd + 1) * HD].astype(BF16)
            s = lax.dot_general(qh, kj, _NT, preferred_element_type=F32) * scale
            (p,) = _softmax_parts([s], sink_ref[hd])
            o_ref[:, hd * HD:(hd + 1) * HD] = jnp.dot(p, vj, preferred_element_type=F32)


def _ctx_attn(sink, q, k, v):
    return pl.pallas_call(
        _ctx_attn_kernel,
        grid=(N_CTX_SEQ,),
        in_specs=[
            pl.BlockSpec(memory_space=pltpu.SMEM),
            pl.BlockSpec((SEQ_CTX, Q_DIM), lambda b: (b, 0)),
            pl.BlockSpec((SEQ_CTX, KV_DIM), lambda b: (b, 0)),
            pl.BlockSpec((SEQ_CTX, KV_DIM), lambda b: (b, 0)),
        ],
        out_specs=pl.BlockSpec((SEQ_CTX, Q_DIM), lambda b: (b, 0)),
        out_shape=jax.ShapeDtypeStruct((T_CTX, Q_DIM), F32),
        compiler_params=_cparams(("parallel",)),
        name="ctx_attn",
    )(sink, q, k, v)


def _rope(x, cos, sin_signed):
    n = x.shape[-1]
    lane = lax.broadcasted_iota(jnp.int32, x.shape, 1)
    up = pltpu.roll(x, n - 16, 1)
    dn = pltpu.roll(x, 16, 1)
    partner = jnp.where((lane & 16) == 0, up, dn)
    return x * cos + partner * sin_signed


QB = 128
KWIN = 3 * QB


def _lat_attn_kernel(sink_ref, q_ref, k_ref, v_ref, ck_ref, cv_ref, cq_ref, sq_ref, ckk_ref, skk_ref, o_ref):
    scale = HD ** -0.5
    qb = pl.program_id(1)
    start = pl.multiple_of(jnp.clip((qb - 1) * QB, 0, SEQ_LAT - KWIN), QB)
    q = _rope(q_ref[...], cq_ref[...], sq_ref[...])
    kw = _rope(k_ref[pl.ds(start, KWIN), :], ckk_ref[pl.ds(start, KWIN), :], skk_ref[pl.ds(start, KWIN), :])
    vw = v_ref[pl.ds(start, KWIN), :]
    qpos = qb * QB + lax.broadcasted_iota(jnp.int32, (QB, KWIN), 0)
    kpos = start + lax.broadcasted_iota(jnp.int32, (QB, KWIN), 1)
    valid = jnp.abs(qpos - kpos) <= WINDOW
    for j in range(N_KV):
        sl = slice(j * HD, (j + 1) * HD)
        kj = kw[:, sl].astype(BF16)
        vj = vw[:, sl].astype(BF16)
        ckj = ck_ref[:, sl].astype(BF16)
        cvj = cv_ref[:, sl].astype(BF16)
        for g in range(GROUPS):
            hd = j * GROUPS + g
            qh = q[:, hd * HD:(hd + 1) * HD].astype(BF16)
            s_ctx = lax.dot_general(qh, ckj, _NT, preferred_element_type=F32) * scale
            s_loc = lax.dot_general(qh, kj, _NT, preferred_element_type=F32) * scale
            s_loc = jnp.where(valid, s_loc, NEG)
            p_ctx, p_loc = _softmax_parts([s_ctx, s_loc], sink_ref[hd])
            o_ref[:, hd * HD:(hd + 1) * HD] = (jnp.dot(p_ctx, cvj, preferred_element_type=F32)
                                               + jnp.dot(p_loc, vj, preferred_element_type=F32))


def _rope_tables():
    pos = jnp.arange(SEQ_LAT)
    rows = (pos // GRID_W).astype(F32)
    cols = (pos % GRID_W).astype(F32)
    quarter = HD // 4
    freqs = ROPE_BASE ** (-jnp.arange(quarter, dtype=F32) / quarter)
    ar = rows[:, None] * freqs[None, :]
    ac = cols[:, None] * freqs[None, :]
    cos = jnp.concatenate([jnp.cos(ar), jnp.cos(ar), jnp.cos(ac), jnp.cos(ac)], axis=-1)
    sin = jnp.concatenate([-jnp.sin(ar), jnp.sin(ar), -jnp.sin(ac), jnp.sin(ac)], axis=-1)
    return cos, sin


def _lat_attn(sink, q, k, v, ck, cv):
    cos, sin = _rope_tables()
    cq, sq = jnp.tile(cos, (1, N_HEADS)), jnp.tile(sin, (1, N_HEADS))
    ckk, skk = jnp.tile(cos, (1, N_KV)), jnp.tile(sin, (1, N_KV))
    b0 = T_CTX // SEQ_LAT
    full = lambda shape: pl.BlockSpec(shape, lambda b, i: (0, 0))
    per_b = lambda rows, cols: pl.BlockSpec((None, rows, cols), lambda b, i: (b, 0, 0))
    tok_b = lambda cols: pl.BlockSpec((None, SEQ_LAT, cols), lambda b, i: (b0 + b, 0, 0))
    out = pl.pallas_call(
        _lat_attn_kernel,
        grid=(N_LAT_SEQ, SEQ_LAT // QB),
        in_specs=[
            pl.BlockSpec(memory_space=pltpu.SMEM),
            pl.BlockSpec((None, QB, Q_DIM), lambda b, i: (b0 + b, i, 0)),
            tok_b(KV_DIM),
            tok_b(KV_DIM),
            per_b(ck.shape[1], KV_DIM),
            per_b(cv.shape[1], KV_DIM),
            pl.BlockSpec((QB, Q_DIM), lambda b, i: (i, 0)),
            pl.BlockSpec((QB, Q_DIM), lambda b, i: (i, 0)),
            full((SEQ_LAT, KV_DIM)),
            full((SEQ_LAT, KV_DIM)),
        ],
        out_specs=pl.BlockSpec((None, QB, Q_DIM), lambda b, i: (b, i, 0)),
        out_shape=jax.ShapeDtypeStruct((N_LAT_SEQ, SEQ_LAT, Q_DIM), F32),
        compiler_params=_cparams(("parallel", "parallel")),
        name="lat_attn",
    )(sink, q.reshape(T // SEQ_LAT, SEQ_LAT, Q_DIM), k.reshape(T // SEQ_LAT, SEQ_LAT, KV_DIM),
      v.reshape(T // SEQ_LAT, SEQ_LAT, KV_DIM), ck, cv, cq, sq, ckk, skk)
    return out.reshape(T_LAT, Q_DIM)


ST = 256
N_ST = T // ST


def _seq_tile_info(i):
    n_ctx = T_CTX // ST
    per_seq = SEQ_LAT // ST
    is_ctx = i < n_ctx
    pos0 = jnp.where(is_ctx, 0, ((i - n_ctx) % per_seq) * ST)
    slen = jnp.where(is_ctx, SEQ_CTX, SEQ_LAT)
    return pos0, slen


def _halo_specs(cols):
    return [
        pl.BlockSpec((ST, cols), lambda i: (jnp.maximum(i - 1, 0), 0)),
        pl.BlockSpec((ST, cols), lambda i: (i, 0)),
        pl.BlockSpec((ST, cols), lambda i: (jnp.minimum(i + 1, N_ST - 1), 0)),
    ]


def _with_halo(prev_ref, cur_ref, next_ref, halo, pos0, slen):
    has_prev = (pos0 > 0).astype(F32)
    has_next = (pos0 + ST < slen).astype(F32)
    return jnp.concatenate([prev_ref[ST - halo:, :] * has_prev, cur_ref[...], next_ref[:halo, :] * has_next], axis=0)


def _poolout_kernel(actx_ref, alat_ref, up_ref, uc_ref, un_ref, x_ref, mod_ref, wp_ref, ps_ref, wo_ref, o_ref):
    i = pl.program_id(0)
    pos0, slen = _seq_tile_info(i)
    halo = 8
    n = ST + 2 * halo
    ext = _with_halo(up_ref, uc_ref, un_ref, halo, pos0, slen)
    t = pos0 + lax.broadcasted_iota(jnp.int32, (ST, 1), 0)
    att = jnp.where(i < T_CTX // ST, actx_ref[...], alat_ref[...])
    out = jnp.dot(att.astype(BF16), wo_ref[:Q_DIM, :], preferred_element_type=F32)
    for g, w in enumerate(POOL_SIZES):
        sl = slice(g * PGD, (g + 1) * PGD)
        e = ext[:, sl]
        s = e + pltpu.roll(e, 1, 0)
        step = 1
        while 2 * step < w:
            s = pltpu.roll(s, step, 0) + pltpu.roll(s, n - step, 0)
            step *= 2
        lo = jnp.maximum(t - w // 2, 0)
        hi = jnp.minimum(t + (w - w // 2), slen)
        mean = s[halo:halo + ST, :] / (hi - lo).astype(F32)
        pooled = mean - e[halo:halo + ST, :]
        mixed = jnp.dot(pooled.astype(BF16), wp_ref[g], preferred_element_type=F32) * ps_ref[:, sl]
        out = out + jnp.dot(mixed.astype(BF16), wo_ref[Q_DIM + g * PGD:Q_DIM + (g + 1) * PGD, :],
                            preferred_element_type=F32)
    o_ref[...] = x_ref[...] + mod_ref[2:3, :] * out


def _poolout(att_ctx, att_lat, u, x, mods, w_pool_b, pool_scale, w_out_b):
    n_ctx = T_CTX // ST
    return pl.pallas_call(
        _poolout_kernel,
        grid=(N_ST,),
        in_specs=[
            pl.BlockSpec((ST, Q_DIM), lambda i: (jnp.minimum(i, n_ctx - 1), 0)),
            pl.BlockSpec((ST, Q_DIM), lambda i: (jnp.maximum(i - n_ctx, 0), 0)),
        ] + _halo_specs(POOL_DIM) + [
            pl.BlockSpec((ST, D), lambda i: (i, 0)),
            pl.BlockSpec((None, 6, D), lambda i: (_group_of_tile(i, ST), 0, 0)),
            pl.BlockSpec((len(POOL_SIZES), PGD, PGD), lambda i: (0, 0, 0)),
            pl.BlockSpec((1, POOL_DIM), lambda i: (0, 0)),
            pl.BlockSpec((Q_DIM + POOL_DIM, D), lambda i: (0, 0)),
        ],
        out_specs=pl.BlockSpec((ST, D), lambda i: (i, 0)),
        out_shape=jax.ShapeDtypeStruct((T, D), F32),
        compiler_params=_cparams(("parallel",)),
        name="poolout",
    )(att_ctx, att_lat, u, u, u, x, mods, w_pool_b, pool_scale, w_out_b)


def _glu_kernel(x_ref, g_ref, mod_ref, w_ref, b_ref, o_ref):
    h = _modnorm(x_ref[...], g_ref[...], mod_ref[0:1, :], mod_ref[1:2, :])
    a = jnp.dot(h.astype(BF16), w_ref[...], preferred_element_type=F32) + b_ref[...]
    o_ref[...] = a[:, :D] * jax.nn.sigmoid(a[:, D:])


def _glu(x, g, mods, w1_b, b1):
    tt = 512
    return pl.pallas_call(
        _glu_kernel,
        grid=(T // tt,),
        in_specs=[
            pl.BlockSpec((tt, D), lambda i: (i, 0)),
            pl.BlockSpec((1, D), lambda i: (0, 0)),
            pl.BlockSpec((None, 6, D), lambda i: (_group_of_tile(i, tt), 0, 0)),
            pl.BlockSpec((D, 2 * D), lambda i: (0, 0)),
            pl.BlockSpec((1, 2 * D), lambda i: (0, 0)),
        ],
        out_specs=pl.BlockSpec((tt, D), lambda i: (i, 0)),
        out_shape=jax.ShapeDtypeStruct((T, D), F32),
        compiler_params=_cparams(("parallel",)),
        name="glu",
    )(x, g, mods, w1_b, b1)


CONV_HALO = 16


def _convout_kernel(ap_ref, ac_ref, an_ref, x_ref, mod_ref, dw_ref, dwb_ref, lg_ref, lb_ref, w2_ref, b2_ref,
                    o_ref, ext_ref):
    i = pl.program_id(0)
    pos0, slen = _seq_tile_info(i)
    ext_ref[...] = _with_halo(ap_ref, ac_ref, an_ref, CONV_HALO, pos0, slen)
    pad = CONV_W // 2
    acc = jnp.zeros((ST, D), F32)
    for k in range(CONV_W):
        acc = acc + ext_ref[pl.ds(CONV_HALO - pad + k, ST), :] * dw_ref[k:k + 1, :]
    a = acc + dwb_ref[...]
    mu = jnp.mean(a, axis=-1, keepdims=True)
    var = jnp.mean(jnp.square(a - mu), axis=-1, keepdims=True)
    y = (a - mu) * lax.rsqrt(var + EPS) * lg_ref[...] + lb_ref[...]
    y = jax.nn.silu(y)
    out = jnp.dot(y.astype(BF16), w2_ref[...], preferred_element_type=F32) + b2_ref[...]
    o_ref[...] = x_ref[...] + mod_ref[2:3, :] * out


def _convout(a, x, mods, dw, dwb, lg, lb, w2_b, b2):
    row = lambda: pl.BlockSpec((1, D), lambda i: (0, 0))
    return pl.pallas_call(
        _convout_kernel,
        grid=(N_ST,),
        in_specs=_halo_specs(D) + [
            pl.BlockSpec((ST, D), lambda i: (i, 0)),
            pl.BlockSpec((None, 6, D), lambda i: (_group_of_tile(i, ST), 0, 0)),
            pl.BlockSpec((CONV_W, D), lambda i: (0, 0)),
            row(), row(), row(),
            pl.BlockSpec((D, D), lambda i: (0, 0)),
            row(),
        ],
        out_specs=pl.BlockSpec((ST, D), lambda i: (i, 0)),
        out_shape=jax.ShapeDtypeStruct((T, D), F32),
        scratch_shapes=[pltpu.VMEM((ST + 2 * CONV_HALO, D), F32)],
        compiler_params=_cparams(("parallel",)),
        name="convout",
    )(a, a, a, x, mods, dw, dwb, lg, lb, w2_b, b2)


SEL_TT = 256
LANES = 128


def _topk16(problems):
    vals = [[] for _ in problems]
    firsts = [[] for _ in problems]
    for r in range(P_TOPK):
        for i, (work_ref, rank_ref) in enumerate(problems):
            shape = work_ref.shape
            iota = lax.broadcasted_iota(jnp.int32, shape, 0).astype(F32)
            s = work_ref[...]
            m = jnp.max(s, axis=0, keepdims=True)
            first = jnp.min(jnp.where(s == m, iota, float(shape[0])), axis=0, keepdims=True)
            hit = iota == first
            pltpu.store(work_ref, jnp.full(shape, -INF, F32), mask=hit)
            if rank_ref is not None:
                pltpu.store(rank_ref, jnp.full(shape, float(r), F32), mask=hit)
            vals[i].append(m)
            firsts[i].append(first)
    return [jnp.concatenate(v, axis=0) for v in vals], firsts


_CAND_BLOCKS = [(0, 0, 8), (0, 8, 8)] + [(r1, 0, P_TOPK // (r1 + 1)) for r1 in range(1, 8)] + [(None, 0, 8)]


N_CAND = 8 * len(_CAND_BLOCKS)
HEADS_PER_TRIP = 4


def _select_kernel(x_ref, g_ref, mod_ref, wq_ref, keys_ref, hb_ref, r_ref, e2_ref, c_ref, e1_ref,
                   q_scr, s_scr, wk_scr, rk_scr, cw_scr, cr_scr, ce_scr):
    h = _modnorm(x_ref[...], g_ref[...], mod_ref[3:4, :], mod_ref[4:5, :])
    hb = h.astype(BF16)
    hb_ref[...] = h.T.astype(BF16)
    q_scr[...] = jnp.dot(hb, wq_ref[...], preferred_element_type=F32).astype(BF16)
    iota8 = lax.broadcasted_iota(jnp.int32, (8, LANES), 0)
    iota_k = lax.broadcasted_iota(jnp.int32, (N_KEYS, LANES), 0).astype(F32)

    n_lc = SEL_TT // LANES

    def head_group(i, carry):
        probs = [(i * HEADS_PER_TRIP + j, j, lc, j * n_lc + lc) for j in range(HEADS_PER_TRIP) for lc in range(n_lc)]
        for j in range(HEADS_PER_TRIP):
            hd = i * HEADS_PER_TRIP + j
            for p in range(2):
                col = pl.multiple_of((hd * 2 + p) * N_KEYS, N_KEYS)
                s_scr[j, p] = lax.dot_general(keys_ref[hd * 2 + p], q_scr[:, pl.ds(col, N_KEYS)], _NT,
                                              preferred_element_type=F32)
        for hd, j, lc, sc in probs:
            ls = slice(lc * LANES, (lc + 1) * LANES)
            wk_scr[sc, 0] = s_scr[j, 0, :, ls]
            wk_scr[sc, 1] = s_scr[j, 1, :, ls]
            rk_scr[sc] = jnp.full((N_KEYS, LANES), float(P_TOPK), F32)
        tops, picked = _topk16([(wk_scr.at[sc, p], rk_scr.at[sc] if p else None)
                                for _, _, _, sc in probs for p in range(2)])
        for n, (hd, j, lc, sc) in enumerate(probs):
            v1, v2 = tops[2 * n], tops[2 * n + 1]
            for b, (r1, r2, nvalid) in enumerate(_CAND_BLOCKS):
                if r1 is None:
                    blk = v1[8:16, :] + v2[0:1, :]
                else:
                    blk = v1[r1:r1 + 1, :] + v2[r2:r2 + 8, :]
                if nvalid < 8:
                    blk = jnp.where(iota8 < nvalid, blk, -INF)
                cw_scr[sc, 8 * b:8 * b + 8, :] = blk
                ce_scr[sc, 8 * b:8 * b + 8, :] = jnp.exp(blk - (v1[0:1, :] + v2[0:1, :]))
            cr_scr[sc] = jnp.full((N_CAND, LANES), float(P_TOPK), F32)
        _topk16([(cw_scr.at[sc], cr_scr.at[sc]) for _, _, _, sc in probs])
        for n, (hd, j, lc, sc) in enumerate(probs):
            ls = slice(lc * LANES, (lc + 1) * LANES)
            v1, v2, picked1 = tops[2 * n], tops[2 * n + 1], picked[2 * n]
            s1 = s_scr[j, 0, :, ls]
            s2 = s_scr[j, 1, :, ls]
            sel = cr_scr[sc] < float(P_TOPK)
            self32 = sel.astype(F32)
            z = jnp.sum(jnp.where(sel, ce_scr[sc], 0.0), axis=0, keepdims=True)
            cnt_rows = [jnp.sum(self32[0:16, :], axis=0, keepdims=True)]
            for b in range(2, 9):
                cnt_rows.append(jnp.sum(self32[8 * b:8 * b + 8, :], axis=0, keepdims=True))
            cnt = jnp.concatenate(cnt_rows + [self32[72:80, :]], axis=0)
            c_ref[hd, lc] = jnp.zeros((N_KEYS, LANES), F32)
            for r in range(P_TOPK):
                pltpu.store(c_ref.at[hd, lc], jnp.broadcast_to(cnt[r:r + 1, :], (N_KEYS, LANES)),
                            mask=iota_k == picked1[r])
            r_ref[hd, :, ls] = rk_scr[sc].astype(BF16)
            e2_ref[hd, :, ls] = jnp.exp(s2 - v2[0:1, :]).astype(BF16)
            e1_ref[hd, lc] = jnp.exp(s1 - v1[0:1, :]) * (0.5 / z)
        return carry

    lax.fori_loop(0, P_HEADS // HEADS_PER_TRIP, head_group, 0)


def _select(x, g, mods, wq_b, keys_b):
    tt = SEL_TT
    hk = pl.BlockSpec((P_HEADS, N_KEYS, tt), lambda i: (0, 0, i))
    hs = pl.BlockSpec((P_HEADS, tt // LANES, N_KEYS, LANES), lambda i: (0, i, 0, 0))
    n_slot = HEADS_PER_TRIP * (tt // LANES)
    return pl.pallas_call(
        _select_kernel,
        grid=(T // tt,),
        in_specs=[
            pl.BlockSpec((tt, D), lambda i: (i, 0)),
            pl.BlockSpec((1, D), lambda i: (0, 0)),
            pl.BlockSpec((None, 6, D), lambda i: (_group_of_tile(i, tt), 0, 0)),
            pl.BlockSpec((D, 2 * P_HEADS * N_KEYS), lambda i: (0, 0)),
            pl.BlockSpec((2 * P_HEADS, N_KEYS, N_KEYS), lambda i: (0, 0, 0)),
        ],
        out_specs=[pl.BlockSpec((D, tt), lambda i: (0, i)), hk, hk, hs, hs],
        out_shape=[
            jax.ShapeDtypeStruct((D, T), BF16),
            jax.ShapeDtypeStruct((P_HEADS, N_KEYS, T), BF16),
            jax.ShapeDtypeStruct((P_HEADS, N_KEYS, T), BF16),
            jax.ShapeDtypeStruct((P_HEADS, T // LANES, N_KEYS, LANES), F32),
            jax.ShapeDtypeStruct((P_HEADS, T // LANES, N_KEYS, LANES), F32),
        ],
        scratch_shapes=[
            pltpu.VMEM((tt, 2 * P_HEADS * N_KEYS), BF16),
            pltpu.VMEM((HEADS_PER_TRIP, 2, N_KEYS, tt), F32),
            pltpu.VMEM((n_slot, 2, N_KEYS, LANES), F32),
            pltpu.VMEM((n_slot, N_KEYS, LANES), F32),
            pltpu.VMEM((n_slot, N_CAND, LANES), F32),
            pltpu.VMEM((n_slot, N_CAND, LANES), F32),
            pltpu.VMEM((n_slot, N_CAND, LANES), F32),
        ],
        compiler_params=_cparams(("parallel",)),
        name="peer_select",
    )(x, g, mods, wq_b, keys_b)


DT = 512
DE = 1024
I1_PER = DE // N_KEYS
PACK = 16
DSUB = 256


def _row_replicated(ref, hd, sub, ii):
    slabs = [jnp.broadcast_to(ref[hd, sub * (DSUB // LANES) + s, ii:ii + 1, :], (PACK, LANES))
             for s in range(DSUB // LANES)]
    return jnp.concatenate(slabs, axis=1)


def _dense_kernel(final_norm, hb_ref, u_ref, vt_ref, r_ref, e2_ref, c_ref, e1_ref, x_ref, mod_ref, fg_ref, *refs):
    *out_refs, acc_ref, p_ref = refs
    k = pl.program_id(1)

    @pl.when(k == 0)
    def _():
        acc_ref[...] = jnp.zeros_like(acc_ref)

    def scores(sub):
        return jnp.dot(u_ref[...], hb_ref[:, sub * DSUB:(sub + 1) * DSUB], preferred_element_type=F32)

    n_sub = DT // DSUB
    a_next = scores(0)
    for sub in range(n_sub):
        ts = slice(sub * DSUB, (sub + 1) * DSUB)
        a_t = a_next
        if sub + 1 < n_sub:
            a_next = scores(sub + 1)
        act = (a_t * (1.0 + lax.erf(a_t * math.sqrt(0.5)))).astype(BF16)
        for ii in range(I1_PER):
            w = jnp.zeros((N_KEYS // PACK, PACK, DSUB), BF16)
            for hd in range(P_HEADS):
                cb = _row_replicated(c_ref, hd, sub, ii).astype(BF16)
                eb = _row_replicated(e1_ref, hd, sub, ii).astype(BF16)
                w = w + jnp.where(r_ref[hd, :, :, ts] < cb[None], e2_ref[hd, :, :, ts], jnp.zeros((), BF16)) * eb[None]
            rows = slice(ii * N_KEYS, (ii + 1) * N_KEYS)
            p_ref[rows, ts] = act[rows, :] * w.reshape(N_KEYS, DSUB)
        acc_ref[:, ts] += jnp.dot(vt_ref[...], p_ref[:, ts], preferred_element_type=F32)

    @pl.when(k == pl.num_programs(1) - 1)
    def _():
        y = x_ref[...] + mod_ref[5:6, :] * acc_ref[...].T
        if not final_norm:
            out_refs[0][...] = y
        else:
            y = y * lax.rsqrt(jnp.mean(y * y, axis=-1, keepdims=True) + EPS) * fg_ref[...]
            is_ctx = pl.program_id(0) < T_CTX // DT

            @pl.when(is_ctx)
            def _():
                out_refs[0][...] = y

            @pl.when(jnp.logical_not(is_ctx))
            def _():
                out_refs[1][...] = y


def _dense(hb, u_b, vt_b, rk, e2, c, e1, x, mods, fg, final_norm):
    hk_b = pl.BlockSpec((P_HEADS, N_KEYS // PACK, PACK, DT), lambda j, k: (0, 0, 0, j))
    hk_f = pl.BlockSpec((P_HEADS, DT // LANES, I1_PER, LANES), lambda j, k: (0, j, k, 0))
    rk4 = rk.reshape(P_HEADS, N_KEYS // PACK, PACK, T)
    e24 = e2.reshape(P_HEADS, N_KEYS // PACK, PACK, T)
    if final_norm:
        n_ctx = T_CTX // DT
        out_specs = [pl.BlockSpec((DT, D), lambda j, k: (jnp.minimum(j, n_ctx - 1), 0)),
                     pl.BlockSpec((DT, D), lambda j, k: (jnp.maximum(j - n_ctx, 0), 0))]
        out_shape = [jax.ShapeDtypeStruct((T_CTX, D), F32), jax.ShapeDtypeStruct((T_LAT, D), F32)]
    else:
        out_specs = pl.BlockSpec((DT, D), lambda j, k: (j, 0))
        out_shape = jax.ShapeDtypeStruct((T, D), F32)
    return pl.pallas_call(
        functools.partial(_dense_kernel, final_norm),
        grid=(T // DT, N_EXPERTS // DE),
        in_specs=[
            pl.BlockSpec((D, DT), lambda j, k: (0, j)),
            pl.BlockSpec((DE, D), lambda j, k: (k, 0)),
            pl.BlockSpec((D, DE), lambda j, k: (0, k)),
            hk_b, hk_b, hk_f, hk_f,
            pl.BlockSpec((DT, D), lambda j, k: (j, 0)),
            pl.BlockSpec((None, 6, D), lambda j, k: (_group_of_tile(j, DT), 0, 0)),
            pl.BlockSpec((1, D), lambda j, k: (0, 0)),
        ],
        out_specs=out_specs,
        out_shape=out_shape,
        scratch_shapes=[pltpu.VMEM((D, DT), F32), pltpu.VMEM((DE, DT), BF16)],
        compiler_params=_cparams(("arbitrary", "arbitrary")),
        name="peer_dense",
    )(hb, u_b, vt_b, rk4, e24, c, e1, x, mods, fg)


def _peer(x, g, mods, wq, keys, u_tab, v_tab, fg, final_norm):
    hb, rk, e2, c, e1 = _select(x, g, mods, wq.astype(BF16),
                                keys.reshape(2 * P_HEADS, N_KEYS, N_KEYS).astype(BF16))
    return _dense(hb, u_tab.astype(BF16), v_tab.T.astype(BF16), rk, e2, c, e1, x, mods, fg, final_norm)


def kernel(x_prompt, x_sample, c, cache_k, cache_v, c_ctx, mod_w, mod_b, norm_mix_g, norm_ffn_g, w_in, attn_sink, w_pool, pool_scale, w_out, conv_w1, conv_b1, conv_dw, conv_dw_b, conv_ln_g, conv_ln_b, conv_w2, conv_b2, peer_wq, peer_keys, peer_u, peer_v, final_norm_g):
    x = jnp.concatenate([x_prompt.reshape(T_CTX, D), x_sample.reshape(T_LAT, D)], axis=0)
    cvec = jnp.concatenate([c_ctx[None, :], c, jnp.zeros((N_GROUPS - 1 - N_LAT_SEQ, D), F32)], axis=0)
    mods_all = _adaln(cvec, mod_w, mod_b).reshape(mod_w.shape[0], N_GROUPS, 6, D)
    row = lambda a: a.reshape(1, -1)
    fg = row(final_norm_g)

    mods = mods_all[0]
    q, k, v, u = _inproj(x, row(norm_mix_g[0]), mods, w_in[0].astype(BF16))
    att_ctx = _ctx_attn(attn_sink[0], q, k, v)
    n_past = cache_k.shape[2]
    att_lat = _lat_attn(attn_sink[0], q, k, v,
                        cache_k[:, 0].reshape(N_LAT_SEQ, n_past, KV_DIM),
                        cache_v[:, 0].reshape(N_LAT_SEQ, n_past, KV_DIM))
    x = _poolout(att_ctx, att_lat, u, x, mods, w_pool[0].astype(BF16), row(pool_scale[0]), w_out[0].astype(BF16))
    x = _peer(x, row(norm_ffn_g[0]), mods, peer_wq[0], peer_keys[0], peer_u[0], peer_v[0], fg, False)
    state_k = k[:T_CTX].reshape(N_CTX_SEQ, 1, SEQ_CTX, N_KV, HD)
    state_v = v[:T_CTX].reshape(N_CTX_SEQ, 1, SEQ_CTX, N_KV, HD)

    mods = mods_all[1]
    a = _glu(x, row(norm_mix_g[1]), mods, conv_w1[0].astype(BF16), row(conv_b1[0]))
    x = _convout(a, x, mods, conv_dw[0], row(conv_dw_b[0]), row(conv_ln_g[0]), row(conv_ln_b[0]),
                 conv_w2[0].astype(BF16), row(conv_b2[0]))
    y_ctx, y_lat = _peer(x, row(norm_ffn_g[1]), mods, peer_wq[1], peer_keys[1], peer_u[1], peer_v[1], fg, True)

    y_prompt = y_ctx.reshape(N_CTX_SEQ, SEQ_CTX, D)
    y_sample = y_lat.reshape(N_LAT_SEQ, SEQ_LAT, D)
    return (y_prompt, y_sample, state_k, state_v)
```

```python
import functools
import math

import jax
import jax.numpy as jnp
from jax import lax
from jax.experimental import pallas as pl
from jax.experimental.pallas import tpu as pltpu

F32 = jnp.float32
BF16 = jnp.bfloat16

D = 1024
N_CTX_SEQ = 16
SEQ_CTX = 256
N_LAT_SEQ = 2
SEQ_LAT = 2048
T_CTX = N_CTX_SEQ * SEQ_CTX
T_LAT = N_LAT_SEQ * SEQ_LAT
T = T_CTX + T_LAT
N_GROUPS = 8
GRID_W = 64
ROPE_BASE = 10000.0
N_HEADS = 8
N_KV = 2
GROUPS = N_HEADS // N_KV
HD = 64
WINDOW = 128
Q_DIM = N_HEADS * HD
KV_DIM = N_KV * HD
POOL_SIZES = (2, 4, 8, 16)
POOL_DIM = 512
PGD = 128
IN_DIM = Q_DIM + 2 * KV_DIM + POOL_DIM
CONV_W = 31
N_KEYS = 128
N_EXPERTS = N_KEYS * N_KEYS
P_HEADS = 8
P_TOPK = 16
EPS = 1e-6
NEG = -1e30
INF = float("inf")

VMEM_LIMIT = 56 * 1024 * 1024


def _cparams(sem, flags=None):
    return pltpu.CompilerParams(dimension_semantics=sem, vmem_limit_bytes=VMEM_LIMIT, flags=flags)


def _group_of_tile(i, tile):
    n_ctx = T_CTX // tile
    per_seq = SEQ_LAT // tile
    return jnp.where(i < n_ctx, 0, 1 + (i - n_ctx) // per_seq)


def _modnorm(x, g, shift, scale):
    y = x * lax.rsqrt(jnp.mean(x * x, axis=-1, keepdims=True) + EPS)
    y = y * g
    return y * (1 + scale) + shift


def _adaln_kernel(cv_ref, w_ref, b_ref, o_ref):
    a = jax.nn.silu(cv_ref[...]).astype(BF16)
    o_ref[...] = jnp.dot(a, w_ref[...].astype(BF16), preferred_element_type=F32) + b_ref[...]


def _adaln(cvec, mod_w, mod_b):
    L, _, n6 = mod_w.shape
    tn = 1536
    return pl.pallas_call(
        _adaln_kernel,
        grid=(L, n6 // tn),
        in_specs=[
            pl.BlockSpec((N_GROUPS, D), lambda l, n: (0, 0)),
            pl.BlockSpec((None, D, tn), lambda l, n: (l, 0, n)),
            pl.BlockSpec((None, 1, tn), lambda l, n: (l, 0, n)),
        ],
        out_specs=pl.BlockSpec((None, N_GROUPS, tn), lambda l, n: (l, 0, n)),
        out_shape=jax.ShapeDtypeStruct((L, N_GROUPS, n6), F32),
        compiler_params=_cparams(("parallel", "parallel")),
        name="adaln",
    )(cvec, mod_w, mod_b.reshape(L, 1, n6))


def _inproj_kernel(x_ref, g_ref, mod_ref, w_ref, q_ref, k_ref, v_ref, u_ref):
    h = _modnorm(x_ref[...], g_ref[...], mod_ref[0:1, :], mod_ref[1:2, :])
    p = jnp.dot(h.astype(BF16), w_ref[...], preferred_element_type=F32)
    q_ref[...] = p[:, :Q_DIM]
    k_ref[...] = p[:, Q_DIM:Q_DIM + KV_DIM]
    v_ref[...] = p[:, Q_DIM + KV_DIM:Q_DIM + 2 * KV_DIM]
    u_ref[...] = p[:, Q_DIM + 2 * KV_DIM:]


def _inproj(x, g, mods, w_in_b):
    tt = 512
    return pl.pallas_call(
        _inproj_kernel,
        grid=(T // tt,),
        in_specs=[
            pl.BlockSpec((tt, D), lambda i: (i, 0)),
            pl.BlockSpec((1, D), lambda i: (0, 0)),
            pl.BlockSpec((None, 6, D), lambda i: (_group_of_tile(i, tt), 0, 0)),
            pl.BlockSpec((D, IN_DIM), lambda i: (0, 0)),
        ],
        out_specs=[
            pl.BlockSpec((tt, Q_DIM), lambda i: (i, 0)),
            pl.BlockSpec((tt, KV_DIM), lambda i: (i, 0)),
            pl.BlockSpec((tt, KV_DIM), lambda i: (i, 0)),
            pl.BlockSpec((tt, POOL_DIM), lambda i: (i, 0)),
        ],
        out_shape=[
            jax.ShapeDtypeStruct((T, Q_DIM), F32),
            jax.ShapeDtypeStruct((T, KV_DIM), F32),
            jax.ShapeDtypeStruct((T, KV_DIM), F32),
            jax.ShapeDtypeStruct((T, POOL_DIM), F32),
        ],
        compiler_params=_cparams(("parallel",)),
        name="inproj",
    )(x, g, mods, w_in_b)


def _softmax_parts(parts, sk):
    m = sk
    for s in parts:
        m = jnp.maximum(m, jnp.max(s, axis=-1, keepdims=True))
    es = [jnp.exp(s - m) for s in parts]
    den = jnp.exp(sk - m)
    for e in es:
        den = den + jnp.sum(e, axis=-1, keepdims=True)
    return [(e / den).astype(BF16) for e in es]


_NT = (((1,), (1,)), ((), ()))


def _ctx_attn_kernel(sink_ref, q_ref, k_ref, v_ref, o_ref):
    scale = HD ** -0.5
    for j in range(N_KV):
        kj = k_ref[:, j * HD:(j + 1) * HD].astype(BF16)
        vj = v_ref[:, j * HD:(j + 1) * HD].astype(BF16)
        for g in range(GROUPS):
            hd = j * GROUPS + g
            qh = q_ref[:, hd * HD:(hd + 1) * HD].astype(BF16)
            s = lax.dot_general(qh, kj, _NT, preferred_element_type=F32) * scale
            (p,) = _softmax_parts([s], sink_ref[hd])
            o_ref[:, hd * HD:(hd + 1) * HD] = jnp.dot(p, vj, preferred_element_type=F32)


def _ctx_attn(sink, q, k, v):
    return pl.pallas_call(
        _ctx_attn_kernel,
        grid=(N_CTX_SEQ,),
        in_specs=[
            pl.BlockSpec(memory_space=pltpu.SMEM),
            pl.BlockSpec((SEQ_CTX, Q_DIM), lambda b: (b, 0)),
            pl.BlockSpec((SEQ_CTX, KV_DIM), lambda b: (b, 0)),
            pl.BlockSpec((SEQ_CTX, KV_DIM), lambda b: (b, 0)),
        ],
        out_specs=pl.BlockSpec((SEQ_CTX, Q_DIM), lambda b: (b, 0)),
        out_shape=jax.ShapeDtypeStruct((T_CTX, Q_DIM), F32),
        compiler_params=_cparams(("parallel",)),
        name="ctx_attn",
    )(sink, q, k, v)


def _rope(x, cos, sin_signed):
    n = x.shape[-1]
    lane = lax.broadcasted_iota(jnp.int32, x.shape, 1)
    up = pltpu.roll(x, n - 16, 1)
    dn = pltpu.roll(x, 16, 1)
    partner = jnp.where((lane & 16) == 0, up, dn)
    return x * cos + partner * sin_signed


QB = 128
KWIN = 3 * QB


def _lat_attn_kernel(sink_ref, q_ref, k_ref, v_ref, ck_ref, cv_ref, cq_ref, sq_ref, ckk_ref, skk_ref, o_ref):
    scale = HD ** -0.5
    qb = pl.program_id(1)
    start = pl.multiple_of(jnp.clip((qb - 1) * QB, 0, SEQ_LAT - KWIN), QB)
    q = _rope(q_ref[...], cq_ref[...], sq_ref[...])
    kw = _rope(k_ref[pl.ds(start, KWIN), :], ckk_ref[pl.ds(start, KWIN), :], skk_ref[pl.ds(start, KWIN), :])
    vw = v_ref[pl.ds(start, KWIN), :]
    qpos = qb * QB + lax.broadcasted_iota(jnp.int32, (QB, KWIN), 0)
    kpos = start + lax.broadcasted_iota(jnp.int32, (QB, KWIN), 1)
    valid = jnp.abs(qpos - kpos) <= WINDOW
    for j in range(N_KV):
        sl = slice(j * HD, (j + 1) * HD)
        kj = kw[:, sl].astype(BF16)
        vj = vw[:, sl].astype(BF16)
        ckj = ck_ref[:, sl].astype(BF16)
        cvj = cv_ref[:, sl].astype(BF16)
        for g in range(GROUPS):
            hd = j * GROUPS + g
            qh = q[:, hd * HD:(hd + 1) * HD].astype(BF16)
            s_ctx = lax.dot_general(qh, ckj, _NT, preferred_element_type=F32) * scale
            s_loc = lax.dot_general(qh, kj, _NT, preferred_element_type=F32) * scale
            s_loc = jnp.where(valid, s_loc, NEG)
            p_ctx, p_loc = _softmax_parts([s_ctx, s_loc], sink_ref[hd])
            o_ref[:, hd * HD:(hd + 1) * HD] = (jnp.dot(p_ctx, cvj, preferred_element_type=F32)
                                               + jnp.dot(p_loc, vj, preferred_element_type=F32))


def _rope_tables():
    pos = jnp.arange(SEQ_LAT)
    rows = (pos // GRID_W).astype(F32)
    cols = (pos % GRID_W).astype(F32)
    quarter = HD // 4
    freqs = ROPE_BASE ** (-jnp.arange(quarter, dtype=F32) / quarter)
    ar = rows[:, None] * freqs[None, :]
    ac = cols[:, None] * freqs[None, :]
    cos = jnp.concatenate([jnp.cos(ar), jnp.cos(ar), jnp.cos(ac), jnp.cos(ac)], axis=-1)
    sin = jnp.concatenate([-jnp.sin(ar), jnp.sin(ar), -jnp.sin(ac), jnp.sin(ac)], axis=-1)
    return cos, sin


def _lat_attn(sink, q, k, v, ck, cv):
    cos, sin = _rope_tables()
    cq, sq = jnp.tile(cos, (1, N_HEADS)), jnp.tile(sin, (1, N_HEADS))
    ckk, skk = jnp.tile(cos, (1, N_KV)), jnp.tile(sin, (1, N_KV))
    b0 = T_CTX // SEQ_LAT
    full = lambda shape: pl.BlockSpec(shape, lambda b, i: (0, 0))
    per_b = lambda rows, cols: pl.BlockSpec((None, rows, cols), lambda b, i: (b, 0, 0))
    tok_b = lambda cols: pl.BlockSpec((None, SEQ_LAT, cols), lambda b, i: (b0 + b, 0, 0))
    out = pl.pallas_call(
        _lat_attn_kernel,
        grid=(N_LAT_SEQ, SEQ_LAT // QB),
        in_specs=[
            pl.BlockSpec(memory_space=pltpu.SMEM),
            pl.BlockSpec((None, QB, Q_DIM), lambda b, i: (b0 + b, i, 0)),
            tok_b(KV_DIM),
            tok_b(KV_DIM),
            per_b(ck.shape[1], KV_DIM),
            per_b(cv.shape[1], KV_DIM),
            pl.BlockSpec((QB, Q_DIM), lambda b, i: (i, 0)),
            pl.BlockSpec((QB, Q_DIM), lambda b, i: (i, 0)),
            full((SEQ_LAT, KV_DIM)),
            full((SEQ_LAT, KV_DIM)),
        ],
        out_specs=pl.BlockSpec((None, QB, Q_DIM), lambda b, i: (b, i, 0)),
        out_shape=jax.ShapeDtypeStruct((N_LAT_SEQ, SEQ_LAT, Q_DIM), F32),
        compiler_params=_cparams(("parallel", "parallel")),
        name="lat_attn",
    )(sink, q.reshape(T // SEQ_LAT, SEQ_LAT, Q_DIM), k.reshape(T // SEQ_LAT, SEQ_LAT, KV_DIM),
      v.reshape(T // SEQ_LAT, SEQ_LAT, KV_DIM), ck, cv, cq, sq, ckk, skk)
    return out.reshape(T_LAT, Q_DIM)


ST = 256
N_ST = T // ST


def _seq_tile_info(i):
    n_ctx = T_CTX // ST
    per_seq = SEQ_LAT // ST
    is_ctx = i < n_ctx
    pos0 = jnp.where(is_ctx, 0, ((i - n_ctx) % per_seq) * ST)
    slen = jnp.where(is_ctx, SEQ_CTX, SEQ_LAT)
    return pos0, slen


def _halo_specs(cols):
    return [
        pl.BlockSpec((ST, cols), lambda i: (jnp.maximum(i - 1, 0), 0)),
        pl.BlockSpec((ST, cols), lambda i: (i, 0)),
        pl.BlockSpec((ST, cols), lambda i: (jnp.minimum(i + 1, N_ST - 1), 0)),
    ]


def _with_halo(prev_ref, cur_ref, next_ref, halo, pos0, slen):
    has_prev = (pos0 > 0).astype(F32)
    has_next = (pos0 + ST < slen).astype(F32)
    return jnp.concatenate([prev_ref[ST - halo:, :] * has_prev, cur_ref[...], next_ref[:halo, :] * has_next], axis=0)


def _poolout_kernel(actx_ref, alat_ref, up_ref, uc_ref, un_ref, x_ref, mod_ref, wp_ref, ps_ref, wo_ref, o_ref):
    i = pl.program_id(0)
    pos0, slen = _seq_tile_info(i)
    halo = 8
    n = ST + 2 * halo
    ext = _with_halo(up_ref, uc_ref, un_ref, halo, pos0, slen)
    t = pos0 + lax.broadcasted_iota(jnp.int32, (ST, 1), 0)
    att = jnp.where(i < T_CTX // ST, actx_ref[...], alat_ref[...])
    out = jnp.dot(att.astype(BF16), wo_ref[:Q_DIM, :], preferred_element_type=F32)
    for g, w in enumerate(POOL_SIZES):
        sl = slice(g * PGD, (g + 1) * PGD)
        e = ext[:, sl]
        s = e + pltpu.roll(e, 1, 0)
        step = 1
        while 2 * step < w:
            s = pltpu.roll(s, step, 0) + pltpu.roll(s, n - step, 0)
            step *= 2
        lo = jnp.maximum(t - w // 2, 0)
        hi = jnp.minimum(t + (w - w // 2), slen)
        mean = s[halo:halo + ST, :] / (hi - lo).astype(F32)
        pooled = mean - e[halo:halo + ST, :]
        mixed = jnp.dot(pooled.astype(BF16), wp_ref[g], preferred_element_type=F32) * ps_ref[:, sl]
        out = out + jnp.dot(mixed.astype(BF16), wo_ref[Q_DIM + g * PGD:Q_DIM + (g + 1) * PGD, :],
                            preferred_element_type=F32)
    o_ref[...] = x_ref[...] + mod_ref[2:3, :] * out


def _poolout(att_ctx, att_lat, u, x, mods, w_pool_b, pool_scale, w_out_b):
    n_ctx = T_CTX // ST
    return pl.pallas_call(
        _poolout_kernel,
        grid=(N_ST,),
        in_specs=[
            pl.BlockSpec((ST, Q_DIM), lambda i: (jnp.minimum(i, n_ctx - 1), 0)),
            pl.BlockSpec((ST, Q_DIM), lambda i: (jnp.maximum(i - n_ctx, 0), 0)),
        ] + _halo_specs(POOL_DIM) + [
            pl.BlockSpec((ST, D), lambda i: (i, 0)),
            pl.BlockSpec((None, 6, D), lambda i: (_group_of_tile(i, ST), 0, 0)),
            pl.BlockSpec((len(POOL_SIZES), PGD, PGD), lambda i: (0, 0, 0)),
            pl.BlockSpec((1, POOL_DIM), lambda i: (0, 0)),
            pl.BlockSpec((Q_DIM + POOL_DIM, D), lambda i: (0, 0)),
        ],
        out_specs=pl.BlockSpec((ST, D), lambda i: (i, 0)),
        out_shape=jax.ShapeDtypeStruct((T, D), F32),
        compiler_params=_cparams(("parallel",)),
        name="poolout",
    )(att_ctx, att_lat, u, u, u, x, mods, w_pool_b, pool_scale, w_out_b)


def _glu_kernel(x_ref, g_ref, mod_ref, w_ref, b_ref, o_ref):
    h = _modnorm(x_ref[...], g_ref[...], mod_ref[0:1, :], mod_ref[1:2, :])
    a = jnp.dot(h.astype(BF16), w_ref[...], preferred_element_type=F32) + b_ref[...]
    o_ref[...] = a[:, :D] * jax.nn.sigmoid(a[:, D:])


def _glu(x, g, mods, w1_b, b1):
    tt = 512
    return pl.pallas_call(
        _glu_kernel,
        grid=(T // tt,),
        in_specs=[
            pl.BlockSpec((tt, D), lambda i: (i, 0)),
            pl.BlockSpec((1, D), lambda i: (0, 0)),
            pl.BlockSpec((None, 6, D), lambda i: (_group_of_tile(i, tt), 0, 0)),
            pl.BlockSpec((D, 2 * D), lambda i: (0, 0)),
            pl.BlockSpec((1, 2 * D), lambda i: (0, 0)),
        ],
        out_specs=pl.BlockSpec((tt, D), lambda i: (i, 0)),
        out_shape=jax.ShapeDtypeStruct((T, D), F32),
        compiler_params=_cparams(("parallel",)),
        name="glu",
    )(x, g, mods, w1_b, b1)


CONV_HALO = 16
SUBLANES = 8


def _convout_kernel(ap_ref, ac_ref, an_ref, x_ref, mod_ref, dw_ref, dwb_ref, lg_ref, lb_ref, w2_ref, b2_ref,
                    o_ref, ext_ref, sh_ref):
    i = pl.program_id(0)
    pos0, slen = _seq_tile_info(i)
    ext_ref[...] = _with_halo(ap_ref, ac_ref, an_ref, CONV_HALO, pos0, slen)
    n_sh = ST + 2 * CONV_HALO - SUBLANES
    for r in range(1, SUBLANES):
        sh_ref[r - 1] = ext_ref[pl.ds(r, n_sh), :]
    pad = CONV_W // 2
    acc = jnp.zeros((ST, D), F32)
    for k in range(CONV_W):
        off = CONV_HALO - pad + k
        r = off % SUBLANES
        src = ext_ref if r == 0 else sh_ref.at[r - 1]
        acc = acc + src[pl.ds(off - r, ST), :] * dw_ref[k:k + 1, :]
    a = acc + dwb_ref[...]
    mu = jnp.mean(a, axis=-1, keepdims=True)
    var = jnp.mean(jnp.square(a - mu), axis=-1, keepdims=True)
    y = (a - mu) * lax.rsqrt(var + EPS) * lg_ref[...] + lb_ref[...]
    y = jax.nn.silu(y)
    out = jnp.dot(y.astype(BF16), w2_ref[...], preferred_element_type=F32) + b2_ref[...]
    o_ref[...] = x_ref[...] + mod_ref[2:3, :] * out


def _convout(a, x, mods, dw, dwb, lg, lb, w2_b, b2):
    row = lambda: pl.BlockSpec((1, D), lambda i: (0, 0))
    return pl.pallas_call(
        _convout_kernel,
        grid=(N_ST,),
        in_specs=_halo_specs(D) + [
            pl.BlockSpec((ST, D), lambda i: (i, 0)),
            pl.BlockSpec((None, 6, D), lambda i: (_group_of_tile(i, ST), 0, 0)),
            pl.BlockSpec((CONV_W, D), lambda i: (0, 0)),
            row(), row(), row(),
            pl.BlockSpec((D, D), lambda i: (0, 0)),
            row(),
        ],
        out_specs=pl.BlockSpec((ST, D), lambda i: (i, 0)),
        out_shape=jax.ShapeDtypeStruct((T, D), F32),
        scratch_shapes=[pltpu.VMEM((ST + 2 * CONV_HALO, D), F32),
                        pltpu.VMEM((SUBLANES - 1, ST + 2 * CONV_HALO - SUBLANES, D), F32)],
        compiler_params=_cparams(("parallel",)),
        name="convout",
    )(a, a, a, x, mods, dw, dwb, lg, lb, w2_b, b2)


SEL_TT = 256
LANES = 128


def _topk16_exact(problems):
    vals = [[] for _ in problems]
    for r in range(P_TOPK):
        for i, (work_ref, rank_ref) in enumerate(problems):
            shape = work_ref.shape
            iota = lax.broadcasted_iota(jnp.int32, shape, 0).astype(F32)
            s = work_ref[...]
            m = jnp.max(s, axis=0, keepdims=True)
            hit = iota == jnp.min(jnp.where(s == m, iota, float(shape[0])), axis=0, keepdims=True)
            pltpu.store(work_ref, jnp.full(shape, -INF, F32), mask=hit)
            pltpu.store(rank_ref, jnp.full(shape, float(r), F32), mask=hit)
            vals[i].append(m)
    return [jnp.concatenate(v, axis=0) for v in vals]


def _top16_distinct(problems):
    vals = [[] for _ in problems]
    for r in range(P_TOPK):
        for i, (score_ref, rank_ref) in enumerate(problems):
            s = score_ref[...]
            below = s if r == 0 else jnp.where(s < vals[i][-1], s, -INF)
            m = jnp.max(below, axis=0, keepdims=True)
            if rank_ref is not None:
                pltpu.store(rank_ref, jnp.full(s.shape, float(r), F32), mask=s == m)
            vals[i].append(m)
    return [jnp.concatenate(v, axis=0) for v in vals]


def _count_ge(s, thr):
    return jnp.sum((s >= thr).astype(F32), axis=0, keepdims=True)


_CAND_BLOCKS = [(0, 0, 8), (0, 8, 8)] + [(r1, 0, P_TOPK // (r1 + 1)) for r1 in range(1, 8)] + [(None, 0, 8)]


N_CAND = 8 * len(_CAND_BLOCKS)
HEADS_PER_TRIP = 4


def _select_kernel(x_ref, g_ref, mod_ref, wq_ref, keys_ref, hb_ref, r_ref, e2_ref, c_ref, e1_ref,
                   q_scr, s_scr, wk_scr, rk_scr, cw_scr, cr_scr, ce_scr):
    h = _modnorm(x_ref[...], g_ref[...], mod_ref[3:4, :], mod_ref[4:5, :])
    hb = h.astype(BF16)
    hb_ref[...] = h.T.astype(BF16)
    q_scr[...] = jnp.dot(hb, wq_ref[...], preferred_element_type=F32).astype(BF16)
    iota8 = lax.broadcasted_iota(jnp.int32, (8, LANES), 0)

    n_lc = SEL_TT // LANES

    def head_group(i, carry):
        for j in range(HEADS_PER_TRIP):
            hd = i * HEADS_PER_TRIP + j
            for p in range(2):
                col = pl.multiple_of((hd * 2 + p) * N_KEYS, N_KEYS)
                s_scr[j, p] = lax.dot_general(keys_ref[hd * 2 + p], q_scr[:, pl.ds(col, N_KEYS)], _NT,
                                              preferred_element_type=F32)
        tied = select_group(i, exact=False)

        @pl.when(jnp.max(tied) > 0.0)
        def _():
            select_group(i, exact=True)

        return carry

    def select_group(i, exact):
        probs = [(i * HEADS_PER_TRIP + j, j, lc, j * n_lc + lc) for j in range(HEADS_PER_TRIP) for lc in range(n_lc)]
        tied = jnp.zeros((1, LANES), F32)
        for hd, j, lc, sc in probs:
            ls = slice(lc * LANES, (lc + 1) * LANES)
            for p in range(2):
                wk_scr[sc, p] = s_scr[j, p, :, ls]
                rk_scr[sc, p] = jnp.full((N_KEYS, LANES), float(P_TOPK), F32)
        if exact:
            tops = _topk16_exact([(wk_scr.at[sc, p], rk_scr.at[sc, p]) for _, _, _, sc in probs for p in range(2)])
        else:
            tops = _top16_distinct([(wk_scr.at[sc, p], rk_scr.at[sc, p] if p else None)
                                    for _, _, _, sc in probs for p in range(2)])
        for n, (hd, j, lc, sc) in enumerate(probs):
            v1, v2 = tops[2 * n], tops[2 * n + 1]
            for b, (r1, r2, nvalid) in enumerate(_CAND_BLOCKS):
                if r1 is None:
                    blk = v1[8:16, :] + v2[0:1, :]
                else:
                    blk = v1[r1:r1 + 1, :] + v2[r2:r2 + 8, :]
                if nvalid < 8:
                    blk = jnp.where(iota8 < nvalid, blk, -INF)
                cw_scr[sc, 8 * b:8 * b + 8, :] = blk
                ce_scr[sc, 8 * b:8 * b + 8, :] = jnp.exp(blk - (v1[0:1, :] + v2[0:1, :]))
            if exact:
                cr_scr[sc] = jnp.full((N_CAND, LANES), float(P_TOPK), F32)
        if exact:
            _topk16_exact([(cw_scr.at[sc], cr_scr.at[sc]) for _, _, _, sc in probs])
        else:
            ctops = _top16_distinct([(cw_scr.at[sc], None) for _, _, _, sc in probs])
        for n, (hd, j, lc, sc) in enumerate(probs):
            ls = slice(lc * LANES, (lc + 1) * LANES)
            v1, v2 = tops[2 * n], tops[2 * n + 1]
            s1 = s_scr[j, 0, :, ls]
            s2 = s_scr[j, 1, :, ls]
            if exact:
                rank1 = rk_scr[sc, 0]
                picked1 = lambda r: rank1 == float(r)
                sel = cr_scr[sc] < float(P_TOPK)
            else:
                picked1 = lambda r: s1 == v1[r:r + 1, :]
                sel = cw_scr[sc] >= ctops[n][P_TOPK - 1:P_TOPK, :]
                for cnt_ge in (_count_ge(s1, v1[P_TOPK - 1:P_TOPK, :]), _count_ge(s2, v2[P_TOPK - 1:P_TOPK, :]),
                               jnp.sum(sel.astype(F32), axis=0, keepdims=True)):
                    tied = jnp.maximum(tied, (cnt_ge != float(P_TOPK)).astype(F32))
            self32 = sel.astype(F32)
            z = jnp.sum(jnp.where(sel, ce_scr[sc], 0.0), axis=0, keepdims=True)
            cnt_rows = [jnp.sum(self32[0:16, :], axis=0, keepdims=True)]
            for b in range(2, 9):
                cnt_rows.append(jnp.sum(self32[8 * b:8 * b + 8, :], axis=0, keepdims=True))
            cnt = jnp.concatenate(cnt_rows + [self32[72:80, :]], axis=0)
            c_ref[hd, lc] = jnp.zeros((N_KEYS, LANES), F32)
            for r in range(P_TOPK):
                pltpu.store(c_ref.at[hd, lc], jnp.broadcast_to(cnt[r:r + 1, :], (N_KEYS, LANES)),
                            mask=picked1(r))
            r_ref[hd, :, ls] = rk_scr[sc, 1].astype(BF16)
            e2_ref[hd, :, ls] = jnp.exp(s2 - v2[0:1, :]).astype(BF16)
            e1_ref[hd, lc] = jnp.exp(s1 - v1[0:1, :]) * (0.5 / z)
        return tied

    lax.fori_loop(0, P_HEADS // HEADS_PER_TRIP, head_group, 0)


def _select(x, g, mods, wq_b, keys_b):
    tt = SEL_TT
    hk = pl.BlockSpec((P_HEADS, N_KEYS, tt), lambda i: (0, 0, i))
    hs = pl.BlockSpec((P_HEADS, tt // LANES, N_KEYS, LANES), lambda i: (0, i, 0, 0))
    n_slot = HEADS_PER_TRIP * (tt // LANES)
    return pl.pallas_call(
        _select_kernel,
        grid=(T // tt,),
        in_specs=[
            pl.BlockSpec((tt, D), lambda i: (i, 0)),
            pl.BlockSpec((1, D), lambda i: (0, 0)),
            pl.BlockSpec((None, 6, D), lambda i: (_group_of_tile(i, tt), 0, 0)),
            pl.BlockSpec((D, 2 * P_HEADS * N_KEYS), lambda i: (0, 0)),
            pl.BlockSpec((2 * P_HEADS, N_KEYS, N_KEYS), lambda i: (0, 0, 0)),
        ],
        out_specs=[pl.BlockSpec((D, tt), lambda i: (0, i)), hk, hk, hs, hs],
        out_shape=[
            jax.ShapeDtypeStruct((D, T), BF16),
            jax.ShapeDtypeStruct((P_HEADS, N_KEYS, T), BF16),
            jax.ShapeDtypeStruct((P_HEADS, N_KEYS, T), BF16),
            jax.ShapeDtypeStruct((P_HEADS, T // LANES, N_KEYS, LANES), F32),
            jax.ShapeDtypeStruct((P_HEADS, T // LANES, N_KEYS, LANES), F32),
        ],
        scratch_shapes=[
            pltpu.VMEM((tt, 2 * P_HEADS * N_KEYS), BF16),
            pltpu.VMEM((HEADS_PER_TRIP, 2, N_KEYS, tt), F32),
            pltpu.VMEM((n_slot, 2, N_KEYS, LANES), F32),
            pltpu.VMEM((n_slot, 2, N_KEYS, LANES), F32),
            pltpu.VMEM((n_slot, N_CAND, LANES), F32),
            pltpu.VMEM((n_slot, N_CAND, LANES), F32),
            pltpu.VMEM((n_slot, N_CAND, LANES), F32),
        ],
        compiler_params=_cparams(("parallel",)),
        name="peer_select",
    )(x, g, mods, wq_b, keys_b)


DT = 512
DE = 2048
I1_PER = DE // N_KEYS
PACK = 16
DSUB = 256


def _row_replicated(ref, hd, sub, ii):
    slabs = [jnp.broadcast_to(ref[hd, sub * (DSUB // LANES) + s, ii:ii + 1, :], (PACK, LANES))
             for s in range(DSUB // LANES)]
    return jnp.concatenate(slabs, axis=1)


def _dense_kernel(final_norm, hb_ref, u_ref, vt_ref, r_ref, e2_ref, c_ref, e1_ref, x_ref, mod_ref, fg_ref, *refs):
    *out_refs, acc_ref, p_ref = refs
    k = pl.program_id(1)

    @pl.when(k == 0)
    def _():
        acc_ref[...] = jnp.zeros_like(acc_ref)

    def scores(sub):
        return jnp.dot(u_ref[...], hb_ref[:, sub * DSUB:(sub + 1) * DSUB], preferred_element_type=F32)

    n_sub = DT // DSUB
    a_next = scores(0)
    for sub in range(n_sub):
        ts = slice(sub * DSUB, (sub + 1) * DSUB)
        a_t = a_next
        if sub + 1 < n_sub:
            a_next = scores(sub + 1)
        act = (a_t * (1.0 + lax.erf(a_t * math.sqrt(0.5)))).astype(BF16)
        for ii in range(I1_PER):
            w = jnp.zeros((N_KEYS // PACK, PACK, DSUB), BF16)
            for hd in range(P_HEADS):
                cb = _row_replicated(c_ref, hd, sub, ii).astype(BF16)
                eb = _row_replicated(e1_ref, hd, sub, ii).astype(BF16)
                w = w + jnp.where(r_ref[hd, :, :, ts] < cb[None], e2_ref[hd, :, :, ts], jnp.zeros((), BF16)) * eb[None]
            rows = slice(ii * N_KEYS, (ii + 1) * N_KEYS)
            p_ref[rows, ts] = act[rows, :] * w.reshape(N_KEYS, DSUB)
        acc_ref[:, ts] += jnp.dot(vt_ref[...], p_ref[:, ts], preferred_element_type=F32)

    @pl.when(k == pl.num_programs(1) - 1)
    def _():
        y = x_ref[...] + mod_ref[5:6, :] * acc_ref[...].T
        if not final_norm:
            out_refs[0][...] = y
        else:
            y = y * lax.rsqrt(jnp.mean(y * y, axis=-1, keepdims=True) + EPS) * fg_ref[...]
            is_ctx = pl.program_id(0) < T_CTX // DT

            @pl.when(is_ctx)
            def _():
                out_refs[0][...] = y

            @pl.when(jnp.logical_not(is_ctx))
            def _():
                out_refs[1][...] = y


def _dense(layer, hb, u_b, vt_b, rk, e2, c, e1, x, mods, fg, final_norm):
    hk_b = pl.BlockSpec((P_HEADS, N_KEYS // PACK, PACK, DT), lambda j, k: (0, 0, 0, j))
    hk_f = pl.BlockSpec((P_HEADS, DT // LANES, I1_PER, LANES), lambda j, k: (0, j, k, 0))
    rk4 = rk.reshape(P_HEADS, N_KEYS // PACK, PACK, T)
    e24 = e2.reshape(P_HEADS, N_KEYS // PACK, PACK, T)
    if final_norm:
        n_ctx = T_CTX // DT
        out_specs = [pl.BlockSpec((DT, D), lambda j, k: (jnp.minimum(j, n_ctx - 1), 0)),
                     pl.BlockSpec((DT, D), lambda j, k: (jnp.maximum(j - n_ctx, 0), 0))]
        out_shape = [jax.ShapeDtypeStruct((T_CTX, D), F32), jax.ShapeDtypeStruct((T_LAT, D), F32)]
    else:
        out_specs = pl.BlockSpec((DT, D), lambda j, k: (j, 0))
        out_shape = jax.ShapeDtypeStruct((T, D), F32)
    return pl.pallas_call(
        functools.partial(_dense_kernel, final_norm),
        grid=(T // DT, N_EXPERTS // DE),
        in_specs=[
            pl.BlockSpec((D, DT), lambda j, k: (0, j)),
            pl.BlockSpec((None, DE, D), lambda j, k: (layer, k, 0)),
            pl.BlockSpec((None, D, DE), lambda j, k: (layer, 0, k)),
            hk_b, hk_b, hk_f, hk_f,
            pl.BlockSpec((DT, D), lambda j, k: (j, 0)),
            pl.BlockSpec((None, 6, D), lambda j, k: (_group_of_tile(j, DT), 0, 0)),
            pl.BlockSpec((1, D), lambda j, k: (0, 0)),
        ],
        out_specs=out_specs,
        out_shape=out_shape,
        scratch_shapes=[pltpu.VMEM((D, DT), F32), pltpu.VMEM((DE, DT), BF16)],
        compiler_params=_cparams(("arbitrary", "arbitrary")),
        name="peer_dense",
    )(hb, u_b, vt_b, rk4, e24, c, e1, x, mods, fg)


def _peer(layer, x, g, mods, wq, keys, u_b, vt_b, fg, final_norm):
    hb, rk, e2, c, e1 = _select(x, g, mods, wq.astype(BF16),
                                keys.reshape(2 * P_HEADS, N_KEYS, N_KEYS).astype(BF16))
    return _dense(layer, hb, u_b, vt_b, rk, e2, c, e1, x, mods, fg, final_norm)


def kernel(x_prompt, x_sample, c, cache_k, cache_v, c_ctx, mod_w, mod_b, norm_mix_g, norm_ffn_g, w_in, attn_sink, w_pool, pool_scale, w_out, conv_w1, conv_b1, conv_dw, conv_dw_b, conv_ln_g, conv_ln_b, conv_w2, conv_b2, peer_wq, peer_keys, peer_u, peer_v, final_norm_g):
    x = jnp.concatenate([x_prompt.reshape(T_CTX, D), x_sample.reshape(T_LAT, D)], axis=0)
    cvec = jnp.concatenate([c_ctx[None, :], c, jnp.zeros((N_GROUPS - 1 - N_LAT_SEQ, D), F32)], axis=0)
    mods_all = _adaln(cvec, mod_w, mod_b).reshape(mod_w.shape[0], N_GROUPS, 6, D)
    row = lambda a: a.reshape(1, -1)
    fg = row(final_norm_g)

    mods = mods_all[0]
    q, k, v, u = _inproj(x, row(norm_mix_g[0]), mods, w_in[0].astype(BF16))
    att_ctx = _ctx_attn(attn_sink[0], q, k, v)
    n_past = cache_k.shape[2]
    att_lat = _lat_attn(attn_sink[0], q, k, v,
                        cache_k[:, 0].reshape(N_LAT_SEQ, n_past, KV_DIM),
                        cache_v[:, 0].reshape(N_LAT_SEQ, n_past, KV_DIM))
    x = _poolout(att_ctx, att_lat, u, x, mods, w_pool[0].astype(BF16), row(pool_scale[0]), w_out[0].astype(BF16))
    u_b = peer_u.astype(BF16)
    vt_b = jnp.swapaxes(peer_v, 1, 2).astype(BF16)
    x = _peer(0, x, row(norm_ffn_g[0]), mods, peer_wq[0], peer_keys[0], u_b, vt_b, fg, False)
    state_k = k[:T_CTX].reshape(N_CTX_SEQ, 1, SEQ_CTX, N_KV, HD)
    state_v = v[:T_CTX].reshape(N_CTX_SEQ, 1, SEQ_CTX, N_KV, HD)

    mods = mods_all[1]
    a = _glu(x, row(norm_mix_g[1]), mods, conv_w1[0].astype(BF16), row(conv_b1[0]))
    x = _convout(a, x, mods, conv_dw[0], row(conv_dw_b[0]), row(conv_ln_g[0]), row(conv_ln_b[0]),
                 conv_w2[0].astype(BF16), row(conv_b2[0]))
    y_ctx, y_lat = _peer(1, x, row(norm_ffn_g[1]), mods, peer_wq[1], peer_keys[1], u_b, vt_b, fg, True)

    y_prompt = y_ctx.reshape(N_CTX_SEQ, SEQ_CTX, D)
    y_sample = y_lat.reshape(N_LAT_SEQ, SEQ_LAT, D)
    return (y_prompt, y_sample, state_k, state_v)
```

```python
import functools
import math

import jax
import jax.numpy as jnp
from jax import lax
from jax.experimental import pallas as pl
from jax.experimental.pallas import tpu as pltpu

F32 = jnp.float32
BF16 = jnp.bfloat16

D = 1024
N_CTX_SEQ = 16
SEQ_CTX = 256
N_LAT_SEQ = 2
SEQ_LAT = 2048
T_CTX = N_CTX_SEQ * SEQ_CTX
T_LAT = N_LAT_SEQ * SEQ_LAT
T = T_CTX + T_LAT
N_GROUPS = 8
GRID_W = 64
ROPE_BASE = 10000.0
N_HEADS = 8
N_KV = 2
GROUPS = N_HEADS // N_KV
HD = 64
WINDOW = 128
Q_DIM = N_HEADS * HD
KV_DIM = N_KV * HD
POOL_SIZES = (2, 4, 8, 16)
POOL_DIM = 512
PGD = 128
IN_DIM = Q_DIM + 2 * KV_DIM + POOL_DIM
CONV_W = 31
N_KEYS = 128
N_EXPERTS = N_KEYS * N_KEYS
P_HEADS = 8
P_TOPK = 16
EPS = 1e-6
NEG = -1e30
INF = float("inf")

VMEM_LIMIT = 56 * 1024 * 1024


def _cparams(sem, flags=None):
    return pltpu.CompilerParams(dimension_semantics=sem, vmem_limit_bytes=VMEM_LIMIT, flags=flags)


def _group_of_tile(i, tile):
    n_ctx = T_CTX // tile
    per_seq = SEQ_LAT // tile
    return jnp.where(i < n_ctx, 0, 1 + (i - n_ctx) // per_seq)


def _modnorm(x, g, shift, scale):
    y = x * lax.rsqrt(jnp.mean(x * x, axis=-1, keepdims=True) + EPS)
    y = y * g
    return y * (1 + scale) + shift


def _adaln_kernel(cv_ref, w_ref, b_ref, o_ref):
    a = jax.nn.silu(cv_ref[...]).astype(BF16)
    o_ref[...] = jnp.dot(a, w_ref[...].astype(BF16), preferred_element_type=F32) + b_ref[...]


def _adaln(cvec, mod_w, mod_b):
    L, _, n6 = mod_w.shape
    tn = 1536
    return pl.pallas_call(
        _adaln_kernel,
        grid=(L, n6 // tn),
        in_specs=[
            pl.BlockSpec((N_GROUPS, D), lambda l, n: (0, 0)),
            pl.BlockSpec((None, D, tn), lambda l, n: (l, 0, n)),
            pl.BlockSpec((None, 1, tn), lambda l, n: (l, 0, n)),
        ],
        out_specs=pl.BlockSpec((None, N_GROUPS, tn), lambda l, n: (l, 0, n)),
        out_shape=jax.ShapeDtypeStruct((L, N_GROUPS, n6), F32),
        compiler_params=_cparams(("parallel", "parallel")),
        name="adaln",
    )(cvec, mod_w, mod_b.reshape(L, 1, n6))


def _two_part_specs(tile, cols):
    n_ctx = T_CTX // tile
    return [pl.BlockSpec((tile, cols), lambda i: (jnp.minimum(i, n_ctx - 1), 0)),
            pl.BlockSpec((tile, cols), lambda i: (jnp.maximum(i - n_ctx, 0), 0))]


def _two_part_tile(ctx_ref, lat_ref, tile):
    return jnp.where(pl.program_id(0) < T_CTX // tile, ctx_ref[...], lat_ref[...])


def _inproj_kernel(xc_ref, xl_ref, g_ref, mod_ref, w_ref, q_ref, k_ref, v_ref, u_ref):
    h = _modnorm(_two_part_tile(xc_ref, xl_ref, INPROJ_TT), g_ref[...], mod_ref[0:1, :], mod_ref[1:2, :])
    p = jnp.dot(h.astype(BF16), w_ref[...], preferred_element_type=F32)
    q_ref[...] = p[:, :Q_DIM]
    k_ref[...] = p[:, Q_DIM:Q_DIM + KV_DIM]
    v_ref[...] = p[:, Q_DIM + KV_DIM:Q_DIM + 2 * KV_DIM]
    u_ref[...] = p[:, Q_DIM + 2 * KV_DIM:]


INPROJ_TT = 512


def _inproj(x_ctx, x_lat, g, mods, w_in_b):
    tt = INPROJ_TT
    return pl.pallas_call(
        _inproj_kernel,
        grid=(T // tt,),
        in_specs=_two_part_specs(tt, D) + [
            pl.BlockSpec((1, D), lambda i: (0, 0)),
            pl.BlockSpec((None, 6, D), lambda i: (_group_of_tile(i, tt), 0, 0)),
            pl.BlockSpec((D, IN_DIM), lambda i: (0, 0)),
        ],
        out_specs=[
            pl.BlockSpec((tt, Q_DIM), lambda i: (i, 0)),
            pl.BlockSpec((tt, KV_DIM), lambda i: (i, 0)),
            pl.BlockSpec((tt, KV_DIM), lambda i: (i, 0)),
            pl.BlockSpec((tt, POOL_DIM), lambda i: (i, 0)),
        ],
        out_shape=[
            jax.ShapeDtypeStruct((T, Q_DIM), F32),
            jax.ShapeDtypeStruct((T, KV_DIM), F32),
            jax.ShapeDtypeStruct((T, KV_DIM), F32),
            jax.ShapeDtypeStruct((T, POOL_DIM), F32),
        ],
        compiler_params=_cparams(("parallel",)),
        name="inproj",
    )(x_ctx, x_lat, g, mods, w_in_b)


def _softmax_parts(parts, sk):
    m = sk
    for s in parts:
        m = jnp.maximum(m, jnp.max(s, axis=-1, keepdims=True))
    es = [jnp.exp(s - m) for s in parts]
    den = jnp.exp(sk - m)
    for e in es:
        den = den + jnp.sum(e, axis=-1, keepdims=True)
    return [(e / den).astype(BF16) for e in es]


_NT = (((1,), (1,)), ((), ()))


def _ctx_attn_kernel(sink_ref, q_ref, k_ref, v_ref, o_ref):
    scale = HD ** -0.5
    for j in range(N_KV):
        kj = k_ref[:, j * HD:(j + 1) * HD].astype(BF16)
        vj = v_ref[:, j * HD:(j + 1) * HD].astype(BF16)
        for g in range(GROUPS):
            hd = j * GROUPS + g
            qh = q_ref[:, hd * HD:(hd + 1) * HD].astype(BF16)
            s = lax.dot_general(qh, kj, _NT, preferred_element_type=F32) * scale
            (p,) = _softmax_parts([s], sink_ref[hd])
            o_ref[:, hd * HD:(hd + 1) * HD] = jnp.dot(p, vj, preferred_element_type=F32)


def _ctx_attn(sink, q, k, v):
    return pl.pallas_call(
        _ctx_attn_kernel,
        grid=(N_CTX_SEQ,),
        in_specs=[
            pl.BlockSpec(memory_space=pltpu.SMEM),
            pl.BlockSpec((SEQ_CTX, Q_DIM), lambda b: (b, 0)),
            pl.BlockSpec((SEQ_CTX, KV_DIM), lambda b: (b, 0)),
            pl.BlockSpec((SEQ_CTX, KV_DIM), lambda b: (b, 0)),
        ],
        out_specs=pl.BlockSpec((SEQ_CTX, Q_DIM), lambda b: (b, 0)),
        out_shape=jax.ShapeDtypeStruct((T_CTX, Q_DIM), F32),
        compiler_params=_cparams(("parallel",)),
        name="ctx_attn",
    )(sink, q, k, v)


def _rope(x, cos, sin_signed):
    n = x.shape[-1]
    lane = lax.broadcasted_iota(jnp.int32, x.shape, 1)
    up = pltpu.roll(x, n - 16, 1)
    dn = pltpu.roll(x, 16, 1)
    partner = jnp.where((lane & 16) == 0, up, dn)
    return x * cos + partner * sin_signed


QB = 128
KWIN = 3 * QB


def _lat_attn_kernel(sink_ref, q_ref, k_ref, v_ref, ck_ref, cv_ref, cq_ref, sq_ref, ckk_ref, skk_ref, o_ref):
    scale = HD ** -0.5
    qb = pl.program_id(1)
    start = pl.multiple_of(jnp.clip((qb - 1) * QB, 0, SEQ_LAT - KWIN), QB)
    q = _rope(q_ref[...], cq_ref[...], sq_ref[...])
    kw = _rope(k_ref[pl.ds(start, KWIN), :], ckk_ref[pl.ds(start, KWIN), :], skk_ref[pl.ds(start, KWIN), :])
    vw = v_ref[pl.ds(start, KWIN), :]
    rows = GROUPS * QB
    qpos = qb * QB + lax.broadcasted_iota(jnp.int32, (rows, KWIN), 0) % QB
    kpos = start + lax.broadcasted_iota(jnp.int32, (rows, KWIN), 1)
    valid = jnp.abs(qpos - kpos) <= WINDOW
    for j in range(N_KV):
        sl = slice(j * HD, (j + 1) * HD)
        kj = kw[:, sl].astype(BF16)
        vj = vw[:, sl].astype(BF16)
        ckj = ck_ref[:, sl].astype(BF16)
        cvj = cv_ref[:, sl].astype(BF16)
        heads = [j * GROUPS + g for g in range(GROUPS)]
        qs = jnp.concatenate([q[:, hd * HD:(hd + 1) * HD] for hd in heads], axis=0).astype(BF16)
        sk = jnp.concatenate([jnp.full((QB, 1), sink_ref[hd], F32) for hd in heads], axis=0)
        s_ctx = lax.dot_general(qs, ckj, _NT, preferred_element_type=F32) * scale
        s_loc = lax.dot_general(qs, kj, _NT, preferred_element_type=F32) * scale
        s_loc = jnp.where(valid, s_loc, NEG)
        p_ctx, p_loc = _softmax_parts([s_ctx, s_loc], sk)
        o = jnp.dot(p_ctx, cvj, preferred_element_type=F32) + jnp.dot(p_loc, vj, preferred_element_type=F32)
        for g, hd in enumerate(heads):
            o_ref[:, hd * HD:(hd + 1) * HD] = o[g * QB:(g + 1) * QB, :]


def _rope_tables():
    pos = jnp.arange(SEQ_LAT)
    rows = (pos // GRID_W).astype(F32)
    cols = (pos % GRID_W).astype(F32)
    quarter = HD // 4
    freqs = ROPE_BASE ** (-jnp.arange(quarter, dtype=F32) / quarter)
    ar = rows[:, None] * freqs[None, :]
    ac = cols[:, None] * freqs[None, :]
    cos = jnp.concatenate([jnp.cos(ar), jnp.cos(ar), jnp.cos(ac), jnp.cos(ac)], axis=-1)
    sin = jnp.concatenate([-jnp.sin(ar), jnp.sin(ar), -jnp.sin(ac), jnp.sin(ac)], axis=-1)
    return cos, sin


def _lat_attn(sink, q, k, v, ck, cv):
    cos, sin = _rope_tables()
    cq, sq = jnp.tile(cos, (1, N_HEADS)), jnp.tile(sin, (1, N_HEADS))
    ckk, skk = jnp.tile(cos, (1, N_KV)), jnp.tile(sin, (1, N_KV))
    b0 = T_CTX // SEQ_LAT
    full = lambda shape: pl.BlockSpec(shape, lambda b, i: (0, 0))
    per_b = lambda rows, cols: pl.BlockSpec((None, rows, cols), lambda b, i: (b, 0, 0))
    tok_b = lambda cols: pl.BlockSpec((None, SEQ_LAT, cols), lambda b, i: (b0 + b, 0, 0))
    out = pl.pallas_call(
        _lat_attn_kernel,
        grid=(N_LAT_SEQ, SEQ_LAT // QB),
        in_specs=[
            pl.BlockSpec(memory_space=pltpu.SMEM),
            pl.BlockSpec((None, QB, Q_DIM), lambda b, i: (b0 + b, i, 0)),
            tok_b(KV_DIM),
            tok_b(KV_DIM),
            per_b(ck.shape[1], KV_DIM),
            per_b(cv.shape[1], KV_DIM),
            pl.BlockSpec((QB, Q_DIM), lambda b, i: (i, 0)),
            pl.BlockSpec((QB, Q_DIM), lambda b, i: (i, 0)),
            full((SEQ_LAT, KV_DIM)),
            full((SEQ_LAT, KV_DIM)),
        ],
        out_specs=pl.BlockSpec((None, QB, Q_DIM), lambda b, i: (b, i, 0)),
        out_shape=jax.ShapeDtypeStruct((N_LAT_SEQ, SEQ_LAT, Q_DIM), F32),
        compiler_params=_cparams(("parallel", "parallel")),
        name="lat_attn",
    )(sink, q.reshape(T // SEQ_LAT, SEQ_LAT, Q_DIM), k.reshape(T // SEQ_LAT, SEQ_LAT, KV_DIM),
      v.reshape(T // SEQ_LAT, SEQ_LAT, KV_DIM), ck, cv, cq, sq, ckk, skk)
    return out.reshape(T_LAT, Q_DIM)


ST = 256
N_ST = T // ST


def _seq_tile_info(i):
    n_ctx = T_CTX // ST
    per_seq = SEQ_LAT // ST
    is_ctx = i < n_ctx
    pos0 = jnp.where(is_ctx, 0, ((i - n_ctx) % per_seq) * ST)
    slen = jnp.where(is_ctx, SEQ_CTX, SEQ_LAT)
    return pos0, slen


def _halo_specs(cols):
    return [
        pl.BlockSpec((ST, cols), lambda i: (jnp.maximum(i - 1, 0), 0)),
        pl.BlockSpec((ST, cols), lambda i: (i, 0)),
        pl.BlockSpec((ST, cols), lambda i: (jnp.minimum(i + 1, N_ST - 1), 0)),
    ]


def _with_halo(prev_ref, cur_ref, next_ref, halo, pos0, slen):
    has_prev = (pos0 > 0).astype(F32)
    has_next = (pos0 + ST < slen).astype(F32)
    return jnp.concatenate([prev_ref[ST - halo:, :] * has_prev, cur_ref[...], next_ref[:halo, :] * has_next], axis=0)


def _poolout_kernel(actx_ref, alat_ref, up_ref, uc_ref, un_ref, xc_ref, xl_ref, mod_ref, wp_ref, ps_ref, wo_ref,
                    o_ref):
    i = pl.program_id(0)
    pos0, slen = _seq_tile_info(i)
    halo = 8
    n = ST + 2 * halo
    ext = _with_halo(up_ref, uc_ref, un_ref, halo, pos0, slen)
    t = pos0 + lax.broadcasted_iota(jnp.int32, (ST, 1), 0)
    att = _two_part_tile(actx_ref, alat_ref, ST)
    out = jnp.dot(att.astype(BF16), wo_ref[:Q_DIM, :], preferred_element_type=F32)
    for g, w in enumerate(POOL_SIZES):
        sl = slice(g * PGD, (g + 1) * PGD)
        e = ext[:, sl]
        s = e + pltpu.roll(e, 1, 0)
        step = 1
        while 2 * step < w:
            s = pltpu.roll(s, step, 0) + pltpu.roll(s, n - step, 0)
            step *= 2
        lo = jnp.maximum(t - w // 2, 0)
        hi = jnp.minimum(t + (w - w // 2), slen)
        mean = s[halo:halo + ST, :] / (hi - lo).astype(F32)
        pooled = mean - e[halo:halo + ST, :]
        mixed = jnp.dot(pooled.astype(BF16), wp_ref[g], preferred_element_type=F32) * ps_ref[:, sl]
        out = out + jnp.dot(mixed.astype(BF16), wo_ref[Q_DIM + g * PGD:Q_DIM + (g + 1) * PGD, :],
                            preferred_element_type=F32)
    o_ref[...] = _two_part_tile(xc_ref, xl_ref, ST) + mod_ref[2:3, :] * out


def _poolout(att_ctx, att_lat, u, x_ctx, x_lat, mods, w_pool_b, pool_scale, w_out_b):
    return pl.pallas_call(
        _poolout_kernel,
        grid=(N_ST,),
        in_specs=_two_part_specs(ST, Q_DIM) + _halo_specs(POOL_DIM) + _two_part_specs(ST, D) + [
            pl.BlockSpec((None, 6, D), lambda i: (_group_of_tile(i, ST), 0, 0)),
            pl.BlockSpec((len(POOL_SIZES), PGD, PGD), lambda i: (0, 0, 0)),
            pl.BlockSpec((1, POOL_DIM), lambda i: (0, 0)),
            pl.BlockSpec((Q_DIM + POOL_DIM, D), lambda i: (0, 0)),
        ],
        out_specs=pl.BlockSpec((ST, D), lambda i: (i, 0)),
        out_shape=jax.ShapeDtypeStruct((T, D), F32),
        compiler_params=_cparams(("parallel",)),
        name="poolout",
    )(att_ctx, att_lat, u, u, u, x_ctx, x_lat, mods, w_pool_b, pool_scale, w_out_b)


def _glu_kernel(x_ref, g_ref, mod_ref, w_ref, b_ref, o_ref):
    h = _modnorm(x_ref[...], g_ref[...], mod_ref[0:1, :], mod_ref[1:2, :])
    a = jnp.dot(h.astype(BF16), w_ref[...], preferred_element_type=F32) + b_ref[...]
    o_ref[...] = a[:, :D] * jax.nn.sigmoid(a[:, D:])


def _glu(x, g, mods, w1_b, b1):
    tt = 512
    return pl.pallas_call(
        _glu_kernel,
        grid=(T // tt,),
        in_specs=[
            pl.BlockSpec((tt, D), lambda i: (i, 0)),
            pl.BlockSpec((1, D), lambda i: (0, 0)),
            pl.BlockSpec((None, 6, D), lambda i: (_group_of_tile(i, tt), 0, 0)),
            pl.BlockSpec((D, 2 * D), lambda i: (0, 0)),
            pl.BlockSpec((1, 2 * D), lambda i: (0, 0)),
        ],
        out_specs=pl.BlockSpec((tt, D), lambda i: (i, 0)),
        out_shape=jax.ShapeDtypeStruct((T, D), F32),
        compiler_params=_cparams(("parallel",)),
        name="glu",
    )(x, g, mods, w1_b, b1)


CONV_HALO = 16
SUBLANES = 8


def _convout_kernel(ap_ref, ac_ref, an_ref, x_ref, mod_ref, dw_ref, dwb_ref, lg_ref, lb_ref, w2_ref, b2_ref,
                    o_ref, ext_ref, sh_ref):
    i = pl.program_id(0)
    pos0, slen = _seq_tile_info(i)
    ext_ref[...] = _with_halo(ap_ref, ac_ref, an_ref, CONV_HALO, pos0, slen)
    n_sh = ST + 2 * CONV_HALO - SUBLANES
    for r in range(1, SUBLANES):
        sh_ref[r - 1] = ext_ref[pl.ds(r, n_sh), :]
    pad = CONV_W // 2
    acc = jnp.zeros((ST, D), F32)
    for k in range(CONV_W):
        off = CONV_HALO - pad + k
        r = off % SUBLANES
        src = ext_ref if r == 0 else sh_ref.at[r - 1]
        acc = acc + src[pl.ds(off - r, ST), :] * dw_ref[k:k + 1, :]
    a = acc + dwb_ref[...]
    mu = jnp.mean(a, axis=-1, keepdims=True)
    var = jnp.mean(jnp.square(a - mu), axis=-1, keepdims=True)
    y = (a - mu) * lax.rsqrt(var + EPS) * lg_ref[...] + lb_ref[...]
    y = jax.nn.silu(y)
    out = jnp.dot(y.astype(BF16), w2_ref[...], preferred_element_type=F32) + b2_ref[...]
    o_ref[...] = x_ref[...] + mod_ref[2:3, :] * out


def _convout(a, x, mods, dw, dwb, lg, lb, w2_b, b2):
    row = lambda: pl.BlockSpec((1, D), lambda i: (0, 0))
    return pl.pallas_call(
        _convout_kernel,
        grid=(N_ST,),
        in_specs=_halo_specs(D) + [
            pl.BlockSpec((ST, D), lambda i: (i, 0)),
            pl.BlockSpec((None, 6, D), lambda i: (_group_of_tile(i, ST), 0, 0)),
            pl.BlockSpec((CONV_W, D), lambda i: (0, 0)),
            row(), row(), row(),
            pl.BlockSpec((D, D), lambda i: (0, 0)),
            row(),
        ],
        out_specs=pl.BlockSpec((ST, D), lambda i: (i, 0)),
        out_shape=jax.ShapeDtypeStruct((T, D), F32),
        scratch_shapes=[pltpu.VMEM((ST + 2 * CONV_HALO, D), F32),
                        pltpu.VMEM((SUBLANES - 1, ST + 2 * CONV_HALO - SUBLANES, D), F32)],
        compiler_params=_cparams(("parallel",)),
        name="convout",
    )(a, a, a, x, mods, dw, dwb, lg, lb, w2_b, b2)


SEL_TT = 256
LANES = 128


def _topk16_exact(problems):
    vals = [[] for _ in problems]
    for r in range(P_TOPK):
        for i, (work_ref, rank_ref) in enumerate(problems):
            shape = work_ref.shape
            iota = lax.broadcasted_iota(jnp.int32, shape, 0).astype(F32)
            s = work_ref[...]
            m = jnp.max(s, axis=0, keepdims=True)
            hit = iota == jnp.min(jnp.where(s == m, iota, float(shape[0])), axis=0, keepdims=True)
            pltpu.store(work_ref, jnp.full(shape, -INF, F32), mask=hit)
            pltpu.store(rank_ref, jnp.full(shape, float(r), F32), mask=hit)
            vals[i].append(m)
    return [jnp.concatenate(v, axis=0) for v in vals]


def _top16_distinct(problems):
    vals = [[] for _ in problems]
    for r in range(P_TOPK):
        for i, (score_ref, rank_ref) in enumerate(problems):
            s = score_ref[...]
            below = s if r == 0 else jnp.where(s < vals[i][-1], s, -INF)
            m = jnp.max(below, axis=0, keepdims=True)
            if rank_ref is not None:
                pltpu.store(rank_ref, jnp.full(s.shape, float(r), F32), mask=s == m)
            vals[i].append(m)
    return [jnp.concatenate(v, axis=0) for v in vals]


def _compare_exchange(t, hi, lo):
    t[hi], t[lo] = jnp.maximum(t[hi], t[lo]), jnp.minimum(t[hi], t[lo])


def _top16_sorted(scores):
    n = P_TOPK
    t = [scores[SUBLANES * i:SUBLANES * (i + 1), :] for i in range(n)]
    k = 2
    while k <= n:
        j = k // 2
        while j >= 1:
            for i in range(n):
                m = i ^ j
                if m > i:
                    if i & k == 0:
                        _compare_exchange(t, i, m)
                    else:
                        _compare_exchange(t, m, i)
            j //= 2
        k *= 2
    for shift in (4, 2, 1):
        other = [pltpu.roll(x, shift, 0) for x in t]
        t = [jnp.maximum(t[i], other[n - 1 - i]) for i in range(n)]
        j = n // 2
        while j >= 1:
            for i in range(n):
                if i ^ j > i:
                    _compare_exchange(t, i, i ^ j)
            j //= 2
    return jnp.concatenate([x[0:1, :] for x in t], axis=0)


def _count_ge(s, thr):
    return jnp.sum((s >= thr).astype(F32), axis=0, keepdims=True)


_CAND_BLOCKS = [(0, 0, 8), (0, 8, 8)] + [(r1, 0, P_TOPK // (r1 + 1)) for r1 in range(1, 8)] + [(None, 0, 8)]


N_CAND = 8 * len(_CAND_BLOCKS)
HEADS_PER_TRIP = 4


def _select_kernel(x_ref, g_ref, mod_ref, wq_ref, keys_ref, hb_ref, r_ref, e2_ref, c_ref, e1_ref,
                   q_scr, s_scr, wk_scr, rk_scr, cw_scr, cr_scr, ce_scr):
    h = _modnorm(x_ref[...], g_ref[...], mod_ref[3:4, :], mod_ref[4:5, :])
    hb = h.astype(BF16)
    hb_ref[...] = h.T.astype(BF16)
    q_scr[...] = jnp.dot(hb, wq_ref[...], preferred_element_type=F32).astype(BF16)
    iota8 = lax.broadcasted_iota(jnp.int32, (8, LANES), 0)

    n_lc = SEL_TT // LANES

    def head_group(i, carry):
        for j in range(HEADS_PER_TRIP):
            hd = i * HEADS_PER_TRIP + j
            for p in range(2):
                col = pl.multiple_of((hd * 2 + p) * N_KEYS, N_KEYS)
                s_scr[j, p] = lax.dot_general(keys_ref[hd * 2 + p], q_scr[:, pl.ds(col, N_KEYS)], _NT,
                                              preferred_element_type=F32)
        tied = select_group(i, exact=False)

        @pl.when(jnp.max(tied) > 0.0)
        def _():
            select_group(i, exact=True)

        return carry

    def select_group(i, exact):
        probs = [(i * HEADS_PER_TRIP + j, j, lc, j * n_lc + lc) for j in range(HEADS_PER_TRIP) for lc in range(n_lc)]
        tied = jnp.zeros((1, LANES), F32)
        for hd, j, lc, sc in probs:
            ls = slice(lc * LANES, (lc + 1) * LANES)
            for p in range(2):
                wk_scr[sc, p] = s_scr[j, p, :, ls]
                rk_scr[sc, p] = jnp.full((N_KEYS, LANES), float(P_TOPK), F32)
        if exact:
            tops = _topk16_exact([(wk_scr.at[sc, p], rk_scr.at[sc, p]) for _, _, _, sc in probs for p in range(2)])
        else:
            tops = [_top16_sorted(wk_scr[sc, p]) for _, _, _, sc in probs for p in range(2)]
        for n, (hd, j, lc, sc) in enumerate(probs):
            v1, v2 = tops[2 * n], tops[2 * n + 1]
            for b, (r1, r2, nvalid) in enumerate(_CAND_BLOCKS):
                if r1 is None:
                    blk = v1[8:16, :] + v2[0:1, :]
                else:
                    blk = v1[r1:r1 + 1, :] + v2[r2:r2 + 8, :]
                if nvalid < 8:
                    blk = jnp.where(iota8 < nvalid, blk, -INF)
                cw_scr[sc, 8 * b:8 * b + 8, :] = blk
                ce_scr[sc, 8 * b:8 * b + 8, :] = jnp.exp(blk - (v1[0:1, :] + v2[0:1, :]))
            if exact:
                cr_scr[sc] = jnp.full((N_CAND, LANES), float(P_TOPK), F32)
        if exact:
            _topk16_exact([(cw_scr.at[sc], cr_scr.at[sc]) for _, _, _, sc in probs])
        else:
            ctops = _top16_distinct([(cw_scr.at[sc], None) for _, _, _, sc in probs])
        for n, (hd, j, lc, sc) in enumerate(probs):
            ls = slice(lc * LANES, (lc + 1) * LANES)
            v1, v2 = tops[2 * n], tops[2 * n + 1]
            s1 = s_scr[j, 0, :, ls]
            s2 = s_scr[j, 1, :, ls]
            if exact:
                rank1 = rk_scr[sc, 0]
                sel = cr_scr[sc] < float(P_TOPK)
            else:
                sel = cw_scr[sc] >= ctops[n][P_TOPK - 1:P_TOPK, :]
                for cnt_ge in (_count_ge(s1, v1[P_TOPK - 1:P_TOPK, :]), _count_ge(s2, v2[P_TOPK - 1:P_TOPK, :]),
                               jnp.sum(sel.astype(F32), axis=0, keepdims=True)):
                    tied = jnp.maximum(tied, (cnt_ge != float(P_TOPK)).astype(F32))
                for v in (v1, v2):
                    repeats = (v[:P_TOPK - 1, :] == v[1:, :]).astype(F32)
                    tied = jnp.maximum(tied, jnp.max(repeats, axis=0, keepdims=True))
            self32 = sel.astype(F32)
            z = jnp.sum(jnp.where(sel, ce_scr[sc], 0.0), axis=0, keepdims=True)
            cnt_rows = [jnp.sum(self32[0:16, :], axis=0, keepdims=True)]
            for b in range(2, 9):
                cnt_rows.append(jnp.sum(self32[8 * b:8 * b + 8, :], axis=0, keepdims=True))
            cnt = jnp.concatenate(cnt_rows + [self32[72:80, :]], axis=0)
            if exact:
                c_ref[hd, lc] = jnp.zeros((N_KEYS, LANES), F32)
                for r in range(P_TOPK):
                    pltpu.store(c_ref.at[hd, lc], jnp.broadcast_to(cnt[r:r + 1, :], (N_KEYS, LANES)),
                                mask=rank1 == float(r))
                rank2 = rk_scr[sc, 1]
            else:
                kept = jnp.zeros((N_KEYS, LANES), F32)
                rank2 = jnp.full((N_KEYS, LANES), float(P_TOPK), F32)
                for r in range(P_TOPK):
                    kept = jnp.where(s1 == v1[r:r + 1, :], cnt[r:r + 1, :], kept)
                    rank2 = jnp.where(s2 == v2[r:r + 1, :], float(r), rank2)
                c_ref[hd, lc] = kept
            r_ref[hd, :, ls] = rank2.astype(BF16)
            e2_ref[hd, :, ls] = jnp.exp(s2 - v2[0:1, :]).astype(BF16)
            e1_ref[hd, lc] = jnp.exp(s1 - v1[0:1, :]) * (0.5 / z)
        return tied

    lax.fori_loop(0, P_HEADS // HEADS_PER_TRIP, head_group, 0)


def _select(x, g, mods, wq_b, keys_b):
    tt = SEL_TT
    hk = pl.BlockSpec((P_HEADS, N_KEYS, tt), lambda i: (0, 0, i))
    hs = pl.BlockSpec((P_HEADS, tt // LANES, N_KEYS, LANES), lambda i: (0, i, 0, 0))
    n_slot = HEADS_PER_TRIP * (tt // LANES)
    return pl.pallas_call(
        _select_kernel,
        grid=(T // tt,),
        in_specs=[
            pl.BlockSpec((tt, D), lambda i: (i, 0)),
            pl.BlockSpec((1, D), lambda i: (0, 0)),
            pl.BlockSpec((None, 6, D), lambda i: (_group_of_tile(i, tt), 0, 0)),
            pl.BlockSpec((D, 2 * P_HEADS * N_KEYS), lambda i: (0, 0)),
            pl.BlockSpec((2 * P_HEADS, N_KEYS, N_KEYS), lambda i: (0, 0, 0)),
        ],
        out_specs=[pl.BlockSpec((D, tt), lambda i: (0, i)), hk, hk, hs, hs],
        out_shape=[
            jax.ShapeDtypeStruct((D, T), BF16),
            jax.ShapeDtypeStruct((P_HEADS, N_KEYS, T), BF16),
            jax.ShapeDtypeStruct((P_HEADS, N_KEYS, T), BF16),
            jax.ShapeDtypeStruct((P_HEADS, T // LANES, N_KEYS, LANES), F32),
            jax.ShapeDtypeStruct((P_HEADS, T // LANES, N_KEYS, LANES), F32),
        ],
        scratch_shapes=[
            pltpu.VMEM((tt, 2 * P_HEADS * N_KEYS), BF16),
            pltpu.VMEM((HEADS_PER_TRIP, 2, N_KEYS, tt), F32),
            pltpu.VMEM((n_slot, 2, N_KEYS, LANES), F32),
            pltpu.VMEM((n_slot, 2, N_KEYS, LANES), F32),
            pltpu.VMEM((n_slot, N_CAND, LANES), F32),
            pltpu.VMEM((n_slot, N_CAND, LANES), F32),
            pltpu.VMEM((n_slot, N_CAND, LANES), F32),
        ],
        compiler_params=_cparams(("parallel",)),
        name="peer_select",
    )(x, g, mods, wq_b, keys_b)


DT = 512
DE = 2048
I1_PER = DE // N_KEYS
PACK = 16
DSUB = 256


def _row_replicated(ref, hd, sub, ii):
    slabs = [jnp.broadcast_to(ref[hd, sub * (DSUB // LANES) + s, ii:ii + 1, :], (PACK, LANES))
             for s in range(DSUB // LANES)]
    return jnp.concatenate(slabs, axis=1)


def _dense_kernel(final_norm, hb_ref, u_ref, vt_ref, r_ref, e2_ref, c_ref, e1_ref, x_ref, mod_ref, fg_ref, *refs):
    *out_refs, acc_ref, p_ref = refs
    k = pl.program_id(1)

    @pl.when(k == 0)
    def _():
        acc_ref[...] = jnp.zeros_like(acc_ref)

    def scores(sub):
        return jnp.dot(u_ref[...], hb_ref[:, sub * DSUB:(sub + 1) * DSUB], preferred_element_type=F32)

    n_sub = DT // DSUB
    a_next = scores(0)
    for sub in range(n_sub):
        ts = slice(sub * DSUB, (sub + 1) * DSUB)
        a_t = a_next
        if sub + 1 < n_sub:
            a_next = scores(sub + 1)
        act = (a_t * (1.0 + lax.erf(a_t * math.sqrt(0.5)))).astype(BF16)
        for ii in range(I1_PER):
            w = jnp.zeros((N_KEYS // PACK, PACK, DSUB), BF16)
            for hd in range(P_HEADS):
                cb = _row_replicated(c_ref, hd, sub, ii).astype(BF16)
                eb = _row_replicated(e1_ref, hd, sub, ii).astype(BF16)
                w = w + jnp.where(r_ref[hd, :, :, ts] < cb[None], e2_ref[hd, :, :, ts], jnp.zeros((), BF16)) * eb[None]
            rows = slice(ii * N_KEYS, (ii + 1) * N_KEYS)
            p_ref[rows, ts] = act[rows, :] * w.reshape(N_KEYS, DSUB)
        acc_ref[:, ts] += jnp.dot(vt_ref[...], p_ref[:, ts], preferred_element_type=F32)

    @pl.when(k == pl.num_programs(1) - 1)
    def _():
        y = x_ref[...] + mod_ref[5:6, :] * acc_ref[...].T
        if not final_norm:
            out_refs[0][...] = y
        else:
            y = y * lax.rsqrt(jnp.mean(y * y, axis=-1, keepdims=True) + EPS) * fg_ref[...]
            is_ctx = pl.program_id(0) < T_CTX // DT

            @pl.when(is_ctx)
            def _():
                out_refs[0][...] = y

            @pl.when(jnp.logical_not(is_ctx))
            def _():
                out_refs[1][...] = y


def _dense(layer, hb, u_b, vt_b, rk, e2, c, e1, x, mods, fg, final_norm):
    hk_b = pl.BlockSpec((P_HEADS, N_KEYS // PACK, PACK, DT), lambda j, k: (0, 0, 0, j))
    hk_f = pl.BlockSpec((P_HEADS, DT // LANES, I1_PER, LANES), lambda j, k: (0, j, k, 0))
    rk4 = rk.reshape(P_HEADS, N_KEYS // PACK, PACK, T)
    e24 = e2.reshape(P_HEADS, N_KEYS // PACK, PACK, T)
    if final_norm:
        n_ctx = T_CTX // DT
        out_specs = [pl.BlockSpec((DT, D), lambda j, k: (jnp.minimum(j, n_ctx - 1), 0)),
                     pl.BlockSpec((DT, D), lambda j, k: (jnp.maximum(j - n_ctx, 0), 0))]
        out_shape = [jax.ShapeDtypeStruct((T_CTX, D), F32), jax.ShapeDtypeStruct((T_LAT, D), F32)]
    else:
        out_specs = pl.BlockSpec((DT, D), lambda j, k: (j, 0))
        out_shape = jax.ShapeDtypeStruct((T, D), F32)
    return pl.pallas_call(
        functools.partial(_dense_kernel, final_norm),
        grid=(T // DT, N_EXPERTS // DE),
        in_specs=[
            pl.BlockSpec((D, DT), lambda j, k: (0, j)),
            pl.BlockSpec((None, DE, D), lambda j, k: (layer, k, 0)),
            pl.BlockSpec((None, D, DE), lambda j, k: (layer, 0, k)),
            hk_b, hk_b, hk_f, hk_f,
            pl.BlockSpec((DT, D), lambda j, k: (j, 0)),
            pl.BlockSpec((None, 6, D), lambda j, k: (_group_of_tile(j, DT), 0, 0)),
            pl.BlockSpec((1, D), lambda j, k: (0, 0)),
        ],
        out_specs=out_specs,
        out_shape=out_shape,
        scratch_shapes=[pltpu.VMEM((D, DT), F32), pltpu.VMEM((DE, DT), BF16)],
        compiler_params=_cparams(("arbitrary", "arbitrary")),
        name="peer_dense",
    )(hb, u_b, vt_b, rk4, e24, c, e1, x, mods, fg)


def _peer(layer, x, g, mods, wq, keys, u_b, vt_b, fg, final_norm):
    hb, rk, e2, c, e1 = _select(x, g, mods, wq.astype(BF16),
                                keys.reshape(2 * P_HEADS, N_KEYS, N_KEYS).astype(BF16))
    return _dense(layer, hb, u_b, vt_b, rk, e2, c, e1, x, mods, fg, final_norm)


def kernel(x_prompt, x_sample, c, cache_k, cache_v, c_ctx, mod_w, mod_b, norm_mix_g, norm_ffn_g, w_in, attn_sink, w_pool, pool_scale, w_out, conv_w1, conv_b1, conv_dw, conv_dw_b, conv_ln_g, conv_ln_b, conv_w2, conv_b2, peer_wq, peer_keys, peer_u, peer_v, final_norm_g):
    x_ctx, x_lat = x_prompt.reshape(T_CTX, D), x_sample.reshape(T_LAT, D)
    cvec = jnp.concatenate([c_ctx[None, :], c, jnp.zeros((N_GROUPS - 1 - N_LAT_SEQ, D), F32)], axis=0)
    mods_all = _adaln(cvec, mod_w, mod_b).reshape(mod_w.shape[0], N_GROUPS, 6, D)
    row = lambda a: a.reshape(1, -1)
    fg = row(final_norm_g)

    mods = mods_all[0]
    q, k, v, u = _inproj(x_ctx, x_lat, row(norm_mix_g[0]), mods, w_in[0].astype(BF16))
    att_ctx = _ctx_attn(attn_sink[0], q, k, v)
    n_past = cache_k.shape[2]
    att_lat = _lat_attn(attn_sink[0], q, k, v,
                        cache_k[:, 0].reshape(N_LAT_SEQ, n_past, KV_DIM),
                        cache_v[:, 0].reshape(N_LAT_SEQ, n_past, KV_DIM))
    x = _poolout(att_ctx, att_lat, u, x_ctx, x_lat, mods, w_pool[0].astype(BF16), row(pool_scale[0]), w_out[0].astype(BF16))
    u_b = peer_u.astype(BF16)
    vt_b = jnp.swapaxes(peer_v, 1, 2).astype(BF16)
    x = _peer(0, x, row(norm_ffn_g[0]), mods, peer_wq[0], peer_keys[0], u_b, vt_b, fg, False)
    state_k = k[:T_CTX].reshape(N_CTX_SEQ, 1, SEQ_CTX, N_KV, HD)
    state_v = v[:T_CTX].reshape(N_CTX_SEQ, 1, SEQ_CTX, N_KV, HD)

    mods = mods_all[1]
    a = _glu(x, row(norm_mix_g[1]), mods, conv_w1[0].astype(BF16), row(conv_b1[0]))
    x = _convout(a, x, mods, conv_dw[0], row(conv_dw_b[0]), row(conv_ln_g[0]), row(conv_ln_b[0]),
                 conv_w2[0].astype(BF16), row(conv_b2[0]))
    y_ctx, y_lat = _peer(1, x, row(norm_ffn_g[1]), mods, peer_wq[1], peer_keys[1], u_b, vt_b, fg, True)

    y_prompt = y_ctx.reshape(N_CTX_SEQ, SEQ_CTX, D)
    y_sample = y_lat.reshape(N_LAT_SEQ, SEQ_LAT, D)
    return (y_prompt, y_sample, state_k, state_v)
```

```python
import functools
import math

import jax
import jax.numpy as jnp
from jax import lax
from jax.experimental import pallas as pl
from jax.experimental.pallas import tpu as pltpu

F32 = jnp.float32
BF16 = jnp.bfloat16

D = 1024
N_CTX_SEQ = 16
SEQ_CTX = 256
N_LAT_SEQ = 2
SEQ_LAT = 2048
T_CTX = N_CTX_SEQ * SEQ_CTX
T_LAT = N_LAT_SEQ * SEQ_LAT
T = T_CTX + T_LAT
N_GROUPS = 8
GRID_W = 64
ROPE_BASE = 10000.0
N_HEADS = 8
N_KV = 2
GROUPS = N_HEADS // N_KV
HD = 64
WINDOW = 128
Q_DIM = N_HEADS * HD
KV_DIM = N_KV * HD
POOL_SIZES = (2, 4, 8, 16)
POOL_DIM = 512
PGD = 128
IN_DIM = Q_DIM + 2 * KV_DIM + POOL_DIM
CONV_W = 31
N_KEYS = 128
N_EXPERTS = N_KEYS * N_KEYS
P_HEADS = 8
P_TOPK = 16
EPS = 1e-6
NEG = -1e30
INF = float("inf")

VMEM_LIMIT = 56 * 1024 * 1024


def _cparams(sem, flags=None):
    return pltpu.CompilerParams(dimension_semantics=sem, vmem_limit_bytes=VMEM_LIMIT, flags=flags)


def _group_of_tile(i, tile):
    n_ctx = T_CTX // tile
    per_seq = SEQ_LAT // tile
    return jnp.where(i < n_ctx, 0, 1 + (i - n_ctx) // per_seq)


def _modnorm(x, g, shift, scale):
    y = x * lax.rsqrt(jnp.mean(x * x, axis=-1, keepdims=True) + EPS)
    y = y * g
    return y * (1 + scale) + shift


def _adaln_kernel(cv_ref, w_ref, b_ref, o_ref):
    a = jax.nn.silu(cv_ref[...]).astype(BF16)
    o_ref[...] = jnp.dot(a, w_ref[...].astype(BF16), preferred_element_type=F32) + b_ref[...]


def _adaln(cvec, mod_w, mod_b):
    L, _, n6 = mod_w.shape
    tn = 1536
    return pl.pallas_call(
        _adaln_kernel,
        grid=(L, n6 // tn),
        in_specs=[
            pl.BlockSpec((N_GROUPS, D), lambda l, n: (0, 0)),
            pl.BlockSpec((None, D, tn), lambda l, n: (l, 0, n)),
            pl.BlockSpec((None, 1, tn), lambda l, n: (l, 0, n)),
        ],
        out_specs=pl.BlockSpec((None, N_GROUPS, tn), lambda l, n: (l, 0, n)),
        out_shape=jax.ShapeDtypeStruct((L, N_GROUPS, n6), F32),
        compiler_params=_cparams(("parallel", "parallel")),
        name="adaln",
    )(cvec, mod_w, mod_b.reshape(L, 1, n6))


def _two_part_specs(tile, cols):
    n_ctx = T_CTX // tile
    return [pl.BlockSpec((tile, cols), lambda i: (jnp.minimum(i, n_ctx - 1), 0)),
            pl.BlockSpec((tile, cols), lambda i: (jnp.maximum(i - n_ctx, 0), 0))]


def _two_part_tile(ctx_ref, lat_ref, tile):
    return jnp.where(pl.program_id(0) < T_CTX // tile, ctx_ref[...], lat_ref[...])


def _inproj_kernel(xc_ref, xl_ref, g_ref, mod_ref, w_ref, q_ref, k_ref, v_ref, u_ref):
    h = _modnorm(_two_part_tile(xc_ref, xl_ref, INPROJ_TT), g_ref[...], mod_ref[0:1, :], mod_ref[1:2, :])
    p = jnp.dot(h.astype(BF16), w_ref[...], preferred_element_type=F32)
    q_ref[...] = p[:, :Q_DIM]
    k_ref[...] = p[:, Q_DIM:Q_DIM + KV_DIM]
    v_ref[...] = p[:, Q_DIM + KV_DIM:Q_DIM + 2 * KV_DIM]
    u_ref[...] = p[:, Q_DIM + 2 * KV_DIM:]


INPROJ_TT = 512


def _inproj(x_ctx, x_lat, g, mods, w_in_b):
    tt = INPROJ_TT
    return pl.pallas_call(
        _inproj_kernel,
        grid=(T // tt,),
        in_specs=_two_part_specs(tt, D) + [
            pl.BlockSpec((1, D), lambda i: (0, 0)),
            pl.BlockSpec((None, 6, D), lambda i: (_group_of_tile(i, tt), 0, 0)),
            pl.BlockSpec((D, IN_DIM), lambda i: (0, 0)),
        ],
        out_specs=[
            pl.BlockSpec((tt, Q_DIM), lambda i: (i, 0)),
            pl.BlockSpec((tt, KV_DIM), lambda i: (i, 0)),
            pl.BlockSpec((tt, KV_DIM), lambda i: (i, 0)),
            pl.BlockSpec((tt, POOL_DIM), lambda i: (i, 0)),
        ],
        out_shape=[
            jax.ShapeDtypeStruct((T, Q_DIM), F32),
            jax.ShapeDtypeStruct((T, KV_DIM), F32),
            jax.ShapeDtypeStruct((T, KV_DIM), F32),
            jax.ShapeDtypeStruct((T, POOL_DIM), F32),
        ],
        compiler_params=_cparams(("parallel",)),
        name="inproj",
    )(x_ctx, x_lat, g, mods, w_in_b)


def _softmax_parts(parts, sk):
    m = sk
    for s in parts:
        m = jnp.maximum(m, jnp.max(s, axis=-1, keepdims=True))
    es = [jnp.exp(s - m) for s in parts]
    den = jnp.exp(sk - m)
    for e in es:
        den = den + jnp.sum(e, axis=-1, keepdims=True)
    return [(e / den).astype(BF16) for e in es]


_NT = (((1,), (1,)), ((), ()))


def _ctx_attn_kernel(sink_ref, q_ref, k_ref, v_ref, o_ref):
    scale = HD ** -0.5
    for j in range(N_KV):
        kj = k_ref[:, j * HD:(j + 1) * HD].astype(BF16)
        vj = v_ref[:, j * HD:(j + 1) * HD].astype(BF16)
        for g in range(GROUPS):
            hd = j * GROUPS + g
            qh = q_ref[:, hd * HD:(hd + 1) * HD].astype(BF16)
            s = lax.dot_general(qh, kj, _NT, preferred_element_type=F32) * scale
            (p,) = _softmax_parts([s], sink_ref[hd])
            o_ref[:, hd * HD:(hd + 1) * HD] = jnp.dot(p, vj, preferred_element_type=F32)


def _ctx_attn(sink, q, k, v):
    return pl.pallas_call(
        _ctx_attn_kernel,
        grid=(N_CTX_SEQ,),
        in_specs=[
            pl.BlockSpec(memory_space=pltpu.SMEM),
            pl.BlockSpec((SEQ_CTX, Q_DIM), lambda b: (b, 0)),
            pl.BlockSpec((SEQ_CTX, KV_DIM), lambda b: (b, 0)),
            pl.BlockSpec((SEQ_CTX, KV_DIM), lambda b: (b, 0)),
        ],
        out_specs=pl.BlockSpec((SEQ_CTX, Q_DIM), lambda b: (b, 0)),
        out_shape=jax.ShapeDtypeStruct((T_CTX, Q_DIM), F32),
        compiler_params=_cparams(("parallel",)),
        name="ctx_attn",
    )(sink, q, k, v)


def _rope(x, cos, sin_signed):
    n = x.shape[-1]
    lane = lax.broadcasted_iota(jnp.int32, x.shape, 1)
    up = pltpu.roll(x, n - 16, 1)
    dn = pltpu.roll(x, 16, 1)
    partner = jnp.where((lane & 16) == 0, up, dn)
    return x * cos + partner * sin_signed


QB = 128
KWIN = QB + 2 * WINDOW


def _lat_attn_kernel(sink_ref, q_ref, k_ref, v_ref, ck_ref, cv_ref, cq_ref, sq_ref, ckk_ref, skk_ref, o_ref):
    scale = HD ** -0.5
    qb = pl.program_id(1)
    start = pl.multiple_of(jnp.clip(qb * QB - WINDOW, 0, SEQ_LAT - KWIN), WINDOW)
    q = _rope(q_ref[...], cq_ref[...], sq_ref[...])
    kw = _rope(k_ref[pl.ds(start, KWIN), :], ckk_ref[pl.ds(start, KWIN), :], skk_ref[pl.ds(start, KWIN), :])
    vw = v_ref[pl.ds(start, KWIN), :]
    rows = GROUPS * QB
    qpos = qb * QB + lax.broadcasted_iota(jnp.int32, (rows, KWIN), 0) % QB
    kpos = start + lax.broadcasted_iota(jnp.int32, (rows, KWIN), 1)
    valid = jnp.abs(qpos - kpos) <= WINDOW
    for j in range(N_KV):
        sl = slice(j * HD, (j + 1) * HD)
        kj = kw[:, sl].astype(BF16)
        vj = vw[:, sl].astype(BF16)
        ckj = ck_ref[:, sl].astype(BF16)
        cvj = cv_ref[:, sl].astype(BF16)
        heads = [j * GROUPS + g for g in range(GROUPS)]
        qs = jnp.concatenate([q[:, hd * HD:(hd + 1) * HD] for hd in heads], axis=0).astype(BF16)
        sk = jnp.concatenate([jnp.full((QB, 1), sink_ref[hd], F32) for hd in heads], axis=0)
        s_ctx = lax.dot_general(qs, ckj, _NT, preferred_element_type=F32) * scale
        s_loc = lax.dot_general(qs, kj, _NT, preferred_element_type=F32) * scale
        s_loc = jnp.where(valid, s_loc, NEG)
        p_ctx, p_loc = _softmax_parts([s_ctx, s_loc], sk)
        o = jnp.dot(p_ctx, cvj, preferred_element_type=F32) + jnp.dot(p_loc, vj, preferred_element_type=F32)
        for g, hd in enumerate(heads):
            o_ref[:, hd * HD:(hd + 1) * HD] = o[g * QB:(g + 1) * QB, :]


def _rope_tables():
    pos = jnp.arange(SEQ_LAT)
    rows = (pos // GRID_W).astype(F32)
    cols = (pos % GRID_W).astype(F32)
    quarter = HD // 4
    freqs = ROPE_BASE ** (-jnp.arange(quarter, dtype=F32) / quarter)
    ar = rows[:, None] * freqs[None, :]
    ac = cols[:, None] * freqs[None, :]
    cos = jnp.concatenate([jnp.cos(ar), jnp.cos(ar), jnp.cos(ac), jnp.cos(ac)], axis=-1)
    sin = jnp.concatenate([-jnp.sin(ar), jnp.sin(ar), -jnp.sin(ac), jnp.sin(ac)], axis=-1)
    return cos, sin


def _lat_attn(sink, q, k, v, ck, cv):
    cos, sin = _rope_tables()
    cq, sq = jnp.tile(cos, (1, N_HEADS)), jnp.tile(sin, (1, N_HEADS))
    ckk, skk = jnp.tile(cos, (1, N_KV)), jnp.tile(sin, (1, N_KV))
    b0 = T_CTX // SEQ_LAT
    full = lambda shape: pl.BlockSpec(shape, lambda b, i: (0, 0))
    per_b = lambda rows, cols: pl.BlockSpec((None, rows, cols), lambda b, i: (b, 0, 0))
    tok_b = lambda cols: pl.BlockSpec((None, SEQ_LAT, cols), lambda b, i: (b0 + b, 0, 0))
    out = pl.pallas_call(
        _lat_attn_kernel,
        grid=(N_LAT_SEQ, SEQ_LAT // QB),
        in_specs=[
            pl.BlockSpec(memory_space=pltpu.SMEM),
            pl.BlockSpec((None, QB, Q_DIM), lambda b, i: (b0 + b, i, 0)),
            tok_b(KV_DIM),
            tok_b(KV_DIM),
            per_b(ck.shape[1], KV_DIM),
            per_b(cv.shape[1], KV_DIM),
            pl.BlockSpec((QB, Q_DIM), lambda b, i: (i, 0)),
            pl.BlockSpec((QB, Q_DIM), lambda b, i: (i, 0)),
            full((SEQ_LAT, KV_DIM)),
            full((SEQ_LAT, KV_DIM)),
        ],
        out_specs=pl.BlockSpec((None, QB, Q_DIM), lambda b, i: (b, i, 0)),
        out_shape=jax.ShapeDtypeStruct((N_LAT_SEQ, SEQ_LAT, Q_DIM), F32),
        compiler_params=_cparams(("parallel", "parallel")),
        name="lat_attn",
    )(sink, q.reshape(T // SEQ_LAT, SEQ_LAT, Q_DIM), k.reshape(T // SEQ_LAT, SEQ_LAT, KV_DIM),
      v.reshape(T // SEQ_LAT, SEQ_LAT, KV_DIM), ck, cv, cq, sq, ckk, skk)
    return out.reshape(T_LAT, Q_DIM)


ST = 256
N_ST = T // ST


def _seq_tile_info(i):
    n_ctx = T_CTX // ST
    per_seq = SEQ_LAT // ST
    is_ctx = i < n_ctx
    pos0 = jnp.where(is_ctx, 0, ((i - n_ctx) % per_seq) * ST)
    slen = jnp.where(is_ctx, SEQ_CTX, SEQ_LAT)
    return pos0, slen


def _halo_specs(cols):
    return [
        pl.BlockSpec((ST, cols), lambda i: (jnp.maximum(i - 1, 0), 0)),
        pl.BlockSpec((ST, cols), lambda i: (i, 0)),
        pl.BlockSpec((ST, cols), lambda i: (jnp.minimum(i + 1, N_ST - 1), 0)),
    ]


def _with_halo(prev_ref, cur_ref, next_ref, halo, pos0, slen):
    has_prev = (pos0 > 0).astype(F32)
    has_next = (pos0 + ST < slen).astype(F32)
    return jnp.concatenate([prev_ref[ST - halo:, :] * has_prev, cur_ref[...], next_ref[:halo, :] * has_next], axis=0)


def _poolout_kernel(actx_ref, alat_ref, up_ref, uc_ref, un_ref, xc_ref, xl_ref, mod_ref, wp_ref, ps_ref, wo_ref,
                    o_ref):
    i = pl.program_id(0)
    pos0, slen = _seq_tile_info(i)
    halo = 8
    n = ST + 2 * halo
    ext = _with_halo(up_ref, uc_ref, un_ref, halo, pos0, slen)
    t = pos0 + lax.broadcasted_iota(jnp.int32, (ST, 1), 0)
    att = _two_part_tile(actx_ref, alat_ref, ST)
    out = jnp.dot(att.astype(BF16), wo_ref[:Q_DIM, :], preferred_element_type=F32)
    for g, w in enumerate(POOL_SIZES):
        sl = slice(g * PGD, (g + 1) * PGD)
        e = ext[:, sl]
        s = e + pltpu.roll(e, 1, 0)
        step = 1
        while 2 * step < w:
            s = pltpu.roll(s, step, 0) + pltpu.roll(s, n - step, 0)
            step *= 2
        lo = jnp.maximum(t - w // 2, 0)
        hi = jnp.minimum(t + (w - w // 2), slen)
        mean = s[halo:halo + ST, :] / (hi - lo).astype(F32)
        pooled = mean - e[halo:halo + ST, :]
        mixed = jnp.dot(pooled.astype(BF16), wp_ref[g], preferred_element_type=F32) * ps_ref[:, sl]
        out = out + jnp.dot(mixed.astype(BF16), wo_ref[Q_DIM + g * PGD:Q_DIM + (g + 1) * PGD, :],
                            preferred_element_type=F32)
    o_ref[...] = _two_part_tile(xc_ref, xl_ref, ST) + mod_ref[2:3, :] * out


def _poolout(att_ctx, att_lat, u, x_ctx, x_lat, mods, w_pool_b, pool_scale, w_out_b):
    return pl.pallas_call(
        _poolout_kernel,
        grid=(N_ST,),
        in_specs=_two_part_specs(ST, Q_DIM) + _halo_specs(POOL_DIM) + _two_part_specs(ST, D) + [
            pl.BlockSpec((None, 6, D), lambda i: (_group_of_tile(i, ST), 0, 0)),
            pl.BlockSpec((len(POOL_SIZES), PGD, PGD), lambda i: (0, 0, 0)),
            pl.BlockSpec((1, POOL_DIM), lambda i: (0, 0)),
            pl.BlockSpec((Q_DIM + POOL_DIM, D), lambda i: (0, 0)),
        ],
        out_specs=pl.BlockSpec((ST, D), lambda i: (i, 0)),
        out_shape=jax.ShapeDtypeStruct((T, D), F32),
        compiler_params=_cparams(("parallel",)),
        name="poolout",
    )(att_ctx, att_lat, u, u, u, x_ctx, x_lat, mods, w_pool_b, pool_scale, w_out_b)


def _glu_kernel(x_ref, g_ref, mod_ref, w_ref, b_ref, o_ref):
    h = _modnorm(x_ref[...], g_ref[...], mod_ref[0:1, :], mod_ref[1:2, :])
    a = jnp.dot(h.astype(BF16), w_ref[...], preferred_element_type=F32) + b_ref[...]
    o_ref[...] = a[:, :D] * jax.nn.sigmoid(a[:, D:])


def _glu(x, g, mods, w1_b, b1):
    tt = 512
    return pl.pallas_call(
        _glu_kernel,
        grid=(T // tt,),
        in_specs=[
            pl.BlockSpec((tt, D), lambda i: (i, 0)),
            pl.BlockSpec((1, D), lambda i: (0, 0)),
            pl.BlockSpec((None, 6, D), lambda i: (_group_of_tile(i, tt), 0, 0)),
            pl.BlockSpec((D, 2 * D), lambda i: (0, 0)),
            pl.BlockSpec((1, 2 * D), lambda i: (0, 0)),
        ],
        out_specs=pl.BlockSpec((tt, D), lambda i: (i, 0)),
        out_shape=jax.ShapeDtypeStruct((T, D), F32),
        compiler_params=_cparams(("parallel",)),
        name="glu",
    )(x, g, mods, w1_b, b1)


CONV_HALO = 16
SUBLANES = 8


def _convout_kernel(ap_ref, ac_ref, an_ref, x_ref, mod_ref, dw_ref, dwb_ref, lg_ref, lb_ref, w2_ref, b2_ref,
                    o_ref, ext_ref, sh_ref):
    i = pl.program_id(0)
    pos0, slen = _seq_tile_info(i)
    ext_ref[...] = _with_halo(ap_ref, ac_ref, an_ref, CONV_HALO, pos0, slen)
    n_sh = ST + 2 * CONV_HALO - SUBLANES
    for r in range(1, SUBLANES):
        sh_ref[r - 1] = ext_ref[pl.ds(r, n_sh), :]
    pad = CONV_W // 2
    half = ST // 2
    parts = []
    for c0 in range(0, D, 128):
        cs = slice(c0, c0 + 128)
        halves = []
        for r0 in (0, half):
            acc = jnp.zeros((half, 128), F32)
            for k in range(CONV_W):
                off = CONV_HALO - pad + k
                r = off % SUBLANES
                src = ext_ref if r == 0 else sh_ref.at[r - 1]
                acc = acc + src[pl.ds(off - r + r0, half), cs] * dw_ref[k:k + 1, cs]
            halves.append(acc)
        parts.append(jnp.concatenate(halves, axis=0))
    a = jnp.concatenate(parts, axis=1) + dwb_ref[...]
    mu = jnp.mean(a, axis=-1, keepdims=True)
    var = jnp.mean(jnp.square(a - mu), axis=-1, keepdims=True)
    y = (a - mu) * lax.rsqrt(var + EPS) * lg_ref[...] + lb_ref[...]
    y = jax.nn.silu(y)
    out = jnp.dot(y.astype(BF16), w2_ref[...], preferred_element_type=F32) + b2_ref[...]
    o_ref[...] = x_ref[...] + mod_ref[2:3, :] * out


def _convout(a, x, mods, dw, dwb, lg, lb, w2_b, b2):
    row = lambda: pl.BlockSpec((1, D), lambda i: (0, 0))
    return pl.pallas_call(
        _convout_kernel,
        grid=(N_ST,),
        in_specs=_halo_specs(D) + [
            pl.BlockSpec((ST, D), lambda i: (i, 0)),
            pl.BlockSpec((None, 6, D), lambda i: (_group_of_tile(i, ST), 0, 0)),
            pl.BlockSpec((CONV_W, D), lambda i: (0, 0)),
            row(), row(), row(),
            pl.BlockSpec((D, D), lambda i: (0, 0)),
            row(),
        ],
        out_specs=pl.BlockSpec((ST, D), lambda i: (i, 0)),
        out_shape=jax.ShapeDtypeStruct((T, D), F32),
        scratch_shapes=[pltpu.VMEM((ST + 2 * CONV_HALO, D), F32),
                        pltpu.VMEM((SUBLANES - 1, ST + 2 * CONV_HALO - SUBLANES, D), F32)],
        compiler_params=_cparams(("parallel",)),
        name="convout",
    )(a, a, a, x, mods, dw, dwb, lg, lb, w2_b, b2)


SEL_TT = 256
LANES = 128


def _topk16_exact(problems):
    vals = [[] for _ in problems]
    for r in range(P_TOPK):
        for i, (work_ref, rank_ref) in enumerate(problems):
            shape = work_ref.shape
            iota = lax.broadcasted_iota(jnp.int32, shape, 0).astype(F32)
            s = work_ref[...]
            m = jnp.max(s, axis=0, keepdims=True)
            hit = iota == jnp.min(jnp.where(s == m, iota, float(shape[0])), axis=0, keepdims=True)
            pltpu.store(work_ref, jnp.full(shape, -INF, F32), mask=hit)
            pltpu.store(rank_ref, jnp.full(shape, float(r), F32), mask=hit)
            vals[i].append(m)
    return [jnp.concatenate(v, axis=0) for v in vals]


def _compare_exchange(t, hi, lo):
    if t[lo] is None:
        return
    if t[hi] is None:
        t[hi], t[lo] = t[lo], None
        return
    t[hi], t[lo] = jnp.maximum(t[hi], t[lo]), jnp.minimum(t[hi], t[lo])


def _top16_sorted(scores):
    n = P_TOPK
    n_tiles = scores.shape[0] // SUBLANES
    t = [scores[SUBLANES * i:SUBLANES * (i + 1), :] if i < n_tiles else None for i in range(n)]
    k = 2
    while k <= n:
        j = k // 2
        while j >= 1:
            for i in range(n):
                m = i ^ j
                if m > i:
                    if i & k == 0:
                        _compare_exchange(t, i, m)
                    else:
                        _compare_exchange(t, m, i)
            j //= 2
        k *= 2
    for shift in (4, 2, 1):
        other = [None if x is None else pltpu.roll(x, shift, 0) for x in t]

        def larger(a, b):
            return b if a is None else a if b is None else jnp.maximum(a, b)

        t = [larger(t[i], other[n - 1 - i]) for i in range(n)]
        j = n // 2
        while j >= 1:
            for i in range(n):
                if i ^ j > i:
                    _compare_exchange(t, i, i ^ j)
            j //= 2
    return jnp.concatenate([x[0:1, :] for x in t], axis=0)


def _count_ge(s, thr):
    return jnp.sum((s >= thr).astype(F32), axis=0, keepdims=True)


_CAND_BLOCKS = [(0, 0, 8), (0, 8, 8)] + [(r1, 0, P_TOPK // (r1 + 1)) for r1 in range(1, 8)] + [(None, 0, 8)]


N_CAND = 8 * len(_CAND_BLOCKS)
HEADS_PER_TRIP = 4


def _select_kernel(x_ref, g_ref, mod_ref, wq_ref, keys_ref, hb_ref, r_ref, e2_ref, c_ref, e1_ref,
                   q_scr, s_scr, wk_scr, rk_scr, cw_scr, cr_scr, ce_scr):
    h = _modnorm(x_ref[...], g_ref[...], mod_ref[3:4, :], mod_ref[4:5, :])
    hb = h.astype(BF16)
    hb_ref[...] = h.T.astype(BF16)
    q_scr[...] = jnp.dot(hb, wq_ref[...], preferred_element_type=F32).astype(BF16)
    iota8 = lax.broadcasted_iota(jnp.int32, (8, LANES), 0)

    n_lc = SEL_TT // LANES

    def head_group(i, carry):
        for j in range(HEADS_PER_TRIP):
            hd = i * HEADS_PER_TRIP + j
            for p in range(2):
                col = pl.multiple_of((hd * 2 + p) * N_KEYS, N_KEYS)
                s_scr[j, p] = lax.dot_general(keys_ref[hd * 2 + p], q_scr[:, pl.ds(col, N_KEYS)], _NT,
                                              preferred_element_type=F32)
        tied = select_group(i, exact=False)

        @pl.when(jnp.max(tied) > 0.0)
        def _():
            select_group(i, exact=True)

        return carry

    def select_group(i, exact):
        probs = [(i * HEADS_PER_TRIP + j, j, lc, j * n_lc + lc) for j in range(HEADS_PER_TRIP) for lc in range(n_lc)]
        tied = jnp.zeros((1, LANES), F32)
        for hd, j, lc, sc in probs:
            ls = slice(lc * LANES, (lc + 1) * LANES)
            for p in range(2):
                wk_scr[sc, p] = s_scr[j, p, :, ls]
                rk_scr[sc, p] = jnp.full((N_KEYS, LANES), float(P_TOPK), F32)
        if exact:
            tops = _topk16_exact([(wk_scr.at[sc, p], rk_scr.at[sc, p]) for _, _, _, sc in probs for p in range(2)])
        else:
            tops = [_top16_sorted(wk_scr[sc, p]) for _, _, _, sc in probs for p in range(2)]
        for n, (hd, j, lc, sc) in enumerate(probs):
            v1, v2 = tops[2 * n], tops[2 * n + 1]
            for b, (r1, r2, nvalid) in enumerate(_CAND_BLOCKS):
                if r1 is None:
                    blk = v1[8:16, :] + v2[0:1, :]
                else:
                    blk = v1[r1:r1 + 1, :] + v2[r2:r2 + 8, :]
                if nvalid < 8:
                    blk = jnp.where(iota8 < nvalid, blk, -INF)
                cw_scr[sc, 8 * b:8 * b + 8, :] = blk
                ce_scr[sc, 8 * b:8 * b + 8, :] = jnp.exp(blk - (v1[0:1, :] + v2[0:1, :]))
            if exact:
                cr_scr[sc] = jnp.full((N_CAND, LANES), float(P_TOPK), F32)
        if exact:
            _topk16_exact([(cw_scr.at[sc], cr_scr.at[sc]) for _, _, _, sc in probs])
        else:
            ctops = [_top16_sorted(cw_scr[sc]) for _, _, _, sc in probs]
        for n, (hd, j, lc, sc) in enumerate(probs):
            ls = slice(lc * LANES, (lc + 1) * LANES)
            v1, v2 = tops[2 * n], tops[2 * n + 1]
            s1 = s_scr[j, 0, :, ls]
            s2 = s_scr[j, 1, :, ls]
            if exact:
                rank1 = rk_scr[sc, 0]
                sel = cr_scr[sc] < float(P_TOPK)
            else:
                sel = cw_scr[sc] >= ctops[n][P_TOPK - 1:P_TOPK, :]
                for cnt_ge in (_count_ge(s1, v1[P_TOPK - 1:P_TOPK, :]), _count_ge(s2, v2[P_TOPK - 1:P_TOPK, :]),
                               jnp.sum(sel.astype(F32), axis=0, keepdims=True)):
                    tied = jnp.maximum(tied, (cnt_ge != float(P_TOPK)).astype(F32))
                for v in (v1, v2):
                    repeats = (v[:P_TOPK - 1, :] == v[1:, :]).astype(F32)
                    tied = jnp.maximum(tied, jnp.max(repeats, axis=0, keepdims=True))
            self32 = sel.astype(F32)
            z = jnp.sum(jnp.where(sel, ce_scr[sc], 0.0), axis=0, keepdims=True)
            cnt_rows = [jnp.sum(self32[0:16, :], axis=0, keepdims=True)]
            for b in range(2, 9):
                cnt_rows.append(jnp.sum(self32[8 * b:8 * b + 8, :], axis=0, keepdims=True))
            cnt = jnp.concatenate(cnt_rows + [self32[72:80, :]], axis=0)
            if exact:
                c_ref[hd, lc] = jnp.zeros((N_KEYS, LANES), F32)
                for r in range(P_TOPK):
                    pltpu.store(c_ref.at[hd, lc], jnp.broadcast_to(cnt[r:r + 1, :], (N_KEYS, LANES)),
                                mask=rank1 == float(r))
                rank2 = rk_scr[sc, 1]
            else:
                kept = jnp.zeros((N_KEYS, LANES), F32)
                rank2 = jnp.full((N_KEYS, LANES), float(P_TOPK), F32)
                for r in range(P_TOPK):
                    kept = jnp.where(s1 == v1[r:r + 1, :], cnt[r:r + 1, :], kept)
                    rank2 = jnp.where(s2 == v2[r:r + 1, :], float(r), rank2)
                c_ref[hd, lc] = kept
            r_ref[hd, :, ls] = rank2.astype(BF16)
            e2_ref[hd, :, ls] = jnp.exp(s2 - v2[0:1, :]).astype(BF16)
            e1_ref[hd, lc] = jnp.exp(s1 - v1[0:1, :]) * (0.5 / z)
        return tied

    lax.fori_loop(0, P_HEADS // HEADS_PER_TRIP, head_group, 0)


def _select(x, g, mods, wq_b, keys_b):
    tt = SEL_TT
    hk = pl.BlockSpec((P_HEADS, N_KEYS, tt), lambda i: (0, 0, i))
    hs = pl.BlockSpec((P_HEADS, tt // LANES, N_KEYS, LANES), lambda i: (0, i, 0, 0))
    n_slot = HEADS_PER_TRIP * (tt // LANES)
    return pl.pallas_call(
        _select_kernel,
        grid=(T // tt,),
        in_specs=[
            pl.BlockSpec((tt, D), lambda i: (i, 0)),
            pl.BlockSpec((1, D), lambda i: (0, 0)),
            pl.BlockSpec((None, 6, D), lambda i: (_group_of_tile(i, tt), 0, 0)),
            pl.BlockSpec((D, 2 * P_HEADS * N_KEYS), lambda i: (0, 0)),
            pl.BlockSpec((2 * P_HEADS, N_KEYS, N_KEYS), lambda i: (0, 0, 0)),
        ],
        out_specs=[pl.BlockSpec((D, tt), lambda i: (0, i)), hk, hk, hs, hs],
        out_shape=[
            jax.ShapeDtypeStruct((D, T), BF16),
            jax.ShapeDtypeStruct((P_HEADS, N_KEYS, T), BF16),
            jax.ShapeDtypeStruct((P_HEADS, N_KEYS, T), BF16),
            jax.ShapeDtypeStruct((P_HEADS, T // LANES, N_KEYS, LANES), F32),
            jax.ShapeDtypeStruct((P_HEADS, T // LANES, N_KEYS, LANES), F32),
        ],
        scratch_shapes=[
            pltpu.VMEM((tt, 2 * P_HEADS * N_KEYS), BF16),
            pltpu.VMEM((HEADS_PER_TRIP, 2, N_KEYS, tt), F32),
            pltpu.VMEM((n_slot, 2, N_KEYS, LANES), F32),
            pltpu.VMEM((n_slot, 2, N_KEYS, LANES), F32),
            pltpu.VMEM((n_slot, N_CAND, LANES), F32),
            pltpu.VMEM((n_slot, N_CAND, LANES), F32),
            pltpu.VMEM((n_slot, N_CAND, LANES), F32),
        ],
        compiler_params=_cparams(("parallel",)),
        name="peer_select",
    )(x, g, mods, wq_b, keys_b)


DT = 512
DE = 2048
I1_PER = DE // N_KEYS
PACK = 16
DSUB = 256


def _row_replicated(ref, hd, sub, ii):
    slabs = [jnp.broadcast_to(ref[hd, sub * (DSUB // LANES) + s, ii:ii + 1, :], (PACK, LANES))
             for s in range(DSUB // LANES)]
    return jnp.concatenate(slabs, axis=1)


def _dense_kernel(final_norm, hb_ref, u_ref, vt_ref, r_ref, e2_ref, c_ref, e1_ref, x_ref, mod_ref, fg_ref, *refs):
    *out_refs, acc_ref, p_ref = refs
    k = pl.program_id(1)

    @pl.when(k == 0)
    def _():
        acc_ref[...] = jnp.zeros_like(acc_ref)

    def scores(sub):
        return jnp.dot(u_ref[...], hb_ref[:, sub * DSUB:(sub + 1) * DSUB], preferred_element_type=F32)

    n_sub = DT // DSUB
    a_next = scores(0)
    for sub in range(n_sub):
        ts = slice(sub * DSUB, (sub + 1) * DSUB)
        a_t = a_next
        if sub + 1 < n_sub:
            a_next = scores(sub + 1)
        act = (a_t * (1.0 + lax.erf(a_t * math.sqrt(0.5)))).astype(BF16)
        for ii in range(I1_PER):
            w = jnp.zeros((N_KEYS // PACK, PACK, DSUB), BF16)
            for hd in range(P_HEADS):
                cb = _row_replicated(c_ref, hd, sub, ii).astype(BF16)
                eb = _row_replicated(e1_ref, hd, sub, ii).astype(BF16)
                w = w + jnp.where(r_ref[hd, :, :, ts] < cb[None], e2_ref[hd, :, :, ts], jnp.zeros((), BF16)) * eb[None]
            rows = slice(ii * N_KEYS, (ii + 1) * N_KEYS)
            p_ref[rows, ts] = act[rows, :] * w.reshape(N_KEYS, DSUB)
        acc_ref[:, ts] += jnp.dot(vt_ref[...], p_ref[:, ts], preferred_element_type=F32)

    @pl.when(k == pl.num_programs(1) - 1)
    def _():
        y = x_ref[...] + mod_ref[5:6, :] * acc_ref[...].T
        if not final_norm:
            out_refs[0][...] = y
        else:
            y = y * lax.rsqrt(jnp.mean(y * y, axis=-1, keepdims=True) + EPS) * fg_ref[...]
            is_ctx = pl.program_id(0) < T_CTX // DT

            @pl.when(is_ctx)
            def _():
                out_refs[0][...] = y

            @pl.when(jnp.logical_not(is_ctx))
            def _():
                out_refs[1][...] = y


def _dense(layer, hb, u_b, vt_b, rk, e2, c, e1, x, mods, fg, final_norm):
    hk_b = pl.BlockSpec((P_HEADS, N_KEYS // PACK, PACK, DT), lambda j, k: (0, 0, 0, j))
    hk_f = pl.BlockSpec((P_HEADS, DT // LANES, I1_PER, LANES), lambda j, k: (0, j, k, 0))
    rk4 = rk.reshape(P_HEADS, N_KEYS // PACK, PACK, T)
    e24 = e2.reshape(P_HEADS, N_KEYS // PACK, PACK, T)
    if final_norm:
        n_ctx = T_CTX // DT
        out_specs = [pl.BlockSpec((DT, D), lambda j, k: (jnp.minimum(j, n_ctx - 1), 0)),
                     pl.BlockSpec((DT, D), lambda j, k: (jnp.maximum(j - n_ctx, 0), 0))]
        out_shape = [jax.ShapeDtypeStruct((T_CTX, D), F32), jax.ShapeDtypeStruct((T_LAT, D), F32)]
    else:
        out_specs = pl.BlockSpec((DT, D), lambda j, k: (j, 0))
        out_shape = jax.ShapeDtypeStruct((T, D), F32)
    return pl.pallas_call(
        functools.partial(_dense_kernel, final_norm),
        grid=(T // DT, N_EXPERTS // DE),
        in_specs=[
            pl.BlockSpec((D, DT), lambda j, k: (0, j)),
            pl.BlockSpec((None, DE, D), lambda j, k: (layer, k, 0)),
            pl.BlockSpec((None, D, DE), lambda j, k: (layer, 0, k)),
            hk_b, hk_b, hk_f, hk_f,
            pl.BlockSpec((DT, D), lambda j, k: (j, 0)),
            pl.BlockSpec((None, 6, D), lambda j, k: (_group_of_tile(j, DT), 0, 0)),
            pl.BlockSpec((1, D), lambda j, k: (0, 0)),
        ],
        out_specs=out_specs,
        out_shape=out_shape,
        scratch_shapes=[pltpu.VMEM((D, DT), F32), pltpu.VMEM((DE, DT), BF16)],
        compiler_params=_cparams(("arbitrary", "arbitrary")),
        name="peer_dense",
    )(hb, u_b, vt_b, rk4, e24, c, e1, x, mods, fg)


def _transpose_cast_kernel(v_ref, o_ref):
    o_ref[...] = v_ref[...].T.astype(BF16)


def _value_tables(peer_v):
    n_layers = peer_v.shape[0]
    blk = 1024
    return pl.pallas_call(
        _transpose_cast_kernel,
        grid=(n_layers, N_EXPERTS // blk),
        in_specs=[pl.BlockSpec((None, blk, D), lambda l, e: (l, e, 0))],
        out_specs=pl.BlockSpec((None, D, blk), lambda l, e: (l, 0, e)),
        out_shape=jax.ShapeDtypeStruct((n_layers, D, N_EXPERTS), BF16),
        compiler_params=_cparams(("parallel", "parallel")),
        name="value_tables",
    )(peer_v)


def _peer(layer, x, g, mods, wq, keys, u_b, vt_b, fg, final_norm):
    hb, rk, e2, c, e1 = _select(x, g, mods, wq.astype(BF16),
                                keys.reshape(2 * P_HEADS, N_KEYS, N_KEYS).astype(BF16))
    return _dense(layer, hb, u_b, vt_b, rk, e2, c, e1, x, mods, fg, final_norm)


def kernel(x_prompt, x_sample, c, cache_k, cache_v, c_ctx, mod_w, mod_b, norm_mix_g, norm_ffn_g, w_in, attn_sink, w_pool, pool_scale, w_out, conv_w1, conv_b1, conv_dw, conv_dw_b, conv_ln_g, conv_ln_b, conv_w2, conv_b2, peer_wq, peer_keys, peer_u, peer_v, final_norm_g):
    x_ctx, x_lat = x_prompt.reshape(T_CTX, D), x_sample.reshape(T_LAT, D)
    cvec = jnp.concatenate([c_ctx[None, :], c, jnp.zeros((N_GROUPS - 1 - N_LAT_SEQ, D), F32)], axis=0)
    mods_all = _adaln(cvec, mod_w, mod_b).reshape(mod_w.shape[0], N_GROUPS, 6, D)
    row = lambda a: a.reshape(1, -1)
    fg = row(final_norm_g)

    mods = mods_all[0]
    q, k, v, u = _inproj(x_ctx, x_lat, row(norm_mix_g[0]), mods, w_in[0].astype(BF16))
    att_ctx = _ctx_attn(attn_sink[0], q, k, v)
    n_past = cache_k.shape[2]
    att_lat = _lat_attn(attn_sink[0], q, k, v,
                        cache_k[:, 0].reshape(N_LAT_SEQ, n_past, KV_DIM),
                        cache_v[:, 0].reshape(N_LAT_SEQ, n_past, KV_DIM))
    x = _poolout(att_ctx, att_lat, u, x_ctx, x_lat, mods, w_pool[0].astype(BF16), row(pool_scale[0]), w_out[0].astype(BF16))
    u_b = peer_u.astype(BF16)
    vt_b = _value_tables(peer_v)
    x = _peer(0, x, row(norm_ffn_g[0]), mods, peer_wq[0], peer_keys[0], u_b, vt_b, fg, False)
    state_k = k[:T_CTX].reshape(N_CTX_SEQ, 1, SEQ_CTX, N_KV, HD)
    state_v = v[:T_CTX].reshape(N_CTX_SEQ, 1, SEQ_CTX, N_KV, HD)

    mods = mods_all[1]
    a = _glu(x, row(norm_mix_g[1]), mods, conv_w1[0].astype(BF16), row(conv_b1[0]))
    x = _convout(a, x, mods, conv_dw[0], row(conv_dw_b[0]), row(conv_ln_g[0]), row(conv_ln_b[0]),
                 conv_w2[0].astype(BF16), row(conv_b2[0]))
    y_ctx, y_lat = _peer(1, x, row(norm_ffn_g[1]), mods, peer_wq[1], peer_keys[1], u_b, vt_b, fg, True)

    y_prompt = y_ctx.reshape(N_CTX_SEQ, SEQ_CTX, D)
    y_sample = y_lat.reshape(N_LAT_SEQ, SEQ_LAT, D)
    return (y_prompt, y_sample, state_k, state_v)
```

```python
import functools
import math

import jax
import jax.numpy as jnp
from jax import lax
from jax.experimental import pallas as pl
from jax.experimental.pallas import tpu as pltpu

F32 = jnp.float32
BF16 = jnp.bfloat16

D = 1024
N_CTX_SEQ = 16
SEQ_CTX = 256
N_LAT_SEQ = 2
SEQ_LAT = 2048
T_CTX = N_CTX_SEQ * SEQ_CTX
T_LAT = N_LAT_SEQ * SEQ_LAT
T = T_CTX + T_LAT
N_GROUPS = 8
GRID_W = 64
ROPE_BASE = 10000.0
N_HEADS = 8
N_KV = 2
GROUPS = N_HEADS // N_KV
HD = 64
WINDOW = 128
Q_DIM = N_HEADS * HD
KV_DIM = N_KV * HD
POOL_SIZES = (2, 4, 8, 16)
POOL_DIM = 512
PGD = 128
IN_DIM = Q_DIM + 2 * KV_DIM + POOL_DIM
CONV_W = 31
N_KEYS = 128
N_EXPERTS = N_KEYS * N_KEYS
P_HEADS = 8
P_TOPK = 16
EPS = 1e-6
NEG = -1e30
INF = float("inf")

VMEM_LIMIT = 56 * 1024 * 1024


def _cparams(sem, flags=None):
    return pltpu.CompilerParams(dimension_semantics=sem, vmem_limit_bytes=VMEM_LIMIT, flags=flags)


def _group_of_tile(i, tile):
    n_ctx = T_CTX // tile
    per_seq = SEQ_LAT // tile
    return jnp.where(i < n_ctx, 0, 1 + (i - n_ctx) // per_seq)


def _modnorm(x, g, shift, scale):
    y = x * lax.rsqrt(jnp.mean(x * x, axis=-1, keepdims=True) + EPS)
    y = y * g
    return y * (1 + scale) + shift


def _adaln_kernel(cv_ref, w_ref, b_ref, o_ref):
    a = jax.nn.silu(cv_ref[...]).astype(BF16)
    o_ref[...] = jnp.dot(a, w_ref[...].astype(BF16), preferred_element_type=F32) + b_ref[...]


def _adaln(cvec, mod_w, mod_b):
    L, _, n6 = mod_w.shape
    tn = 1536
    return pl.pallas_call(
        _adaln_kernel,
        grid=(L, n6 // tn),
        in_specs=[
            pl.BlockSpec((N_GROUPS, D), lambda l, n: (0, 0)),
            pl.BlockSpec((None, D, tn), lambda l, n: (l, 0, n)),
            pl.BlockSpec((None, 1, tn), lambda l, n: (l, 0, n)),
        ],
        out_specs=pl.BlockSpec((None, N_GROUPS, tn), lambda l, n: (l, 0, n)),
        out_shape=jax.ShapeDtypeStruct((L, N_GROUPS, n6), F32),
        compiler_params=_cparams(("parallel", "parallel")),
        name="adaln",
    )(cvec, mod_w, mod_b.reshape(L, 1, n6))


def _two_part_specs(tile, cols):
    n_ctx = T_CTX // tile
    return [pl.BlockSpec((tile, cols), lambda i: (jnp.minimum(i, n_ctx - 1), 0)),
            pl.BlockSpec((tile, cols), lambda i: (jnp.maximum(i - n_ctx, 0), 0))]


def _two_part_tile(ctx_ref, lat_ref, tile):
    return jnp.where(pl.program_id(0) < T_CTX // tile, ctx_ref[...], lat_ref[...])


def _inproj_kernel(xc_ref, xl_ref, g_ref, mod_ref, w_ref, q_ref, k_ref, v_ref, u_ref):
    h = _modnorm(_two_part_tile(xc_ref, xl_ref, INPROJ_TT), g_ref[...], mod_ref[0:1, :], mod_ref[1:2, :])
    p = jnp.dot(h.astype(BF16), w_ref[...], preferred_element_type=F32)
    q_ref[...] = p[:, :Q_DIM]
    k_ref[...] = p[:, Q_DIM:Q_DIM + KV_DIM]
    v_ref[...] = p[:, Q_DIM + KV_DIM:Q_DIM + 2 * KV_DIM]
    u_ref[...] = p[:, Q_DIM + 2 * KV_DIM:]


INPROJ_TT = 512


def _inproj(x_ctx, x_lat, g, mods, w_in_b):
    tt = INPROJ_TT
    return pl.pallas_call(
        _inproj_kernel,
        grid=(T // tt,),
        in_specs=_two_part_specs(tt, D) + [
            pl.BlockSpec((1, D), lambda i: (0, 0)),
            pl.BlockSpec((None, 6, D), lambda i: (_group_of_tile(i, tt), 0, 0)),
            pl.BlockSpec((D, IN_DIM), lambda i: (0, 0)),
        ],
        out_specs=[
            pl.BlockSpec((tt, Q_DIM), lambda i: (i, 0)),
            pl.BlockSpec((tt, KV_DIM), lambda i: (i, 0)),
            pl.BlockSpec((tt, KV_DIM), lambda i: (i, 0)),
            pl.BlockSpec((tt, POOL_DIM), lambda i: (i, 0)),
        ],
        out_shape=[
            jax.ShapeDtypeStruct((T, Q_DIM), F32),
            jax.ShapeDtypeStruct((T, KV_DIM), F32),
            jax.ShapeDtypeStruct((T, KV_DIM), F32),
            jax.ShapeDtypeStruct((T, POOL_DIM), F32),
        ],
        compiler_params=_cparams(("parallel",)),
        name="inproj",
    )(x_ctx, x_lat, g, mods, w_in_b)


def _softmax_parts(parts, sk):
    m = sk
    for s in parts:
        m = jnp.maximum(m, jnp.max(s, axis=-1, keepdims=True))
    es = [jnp.exp(s - m) for s in parts]
    den = jnp.exp(sk - m)
    for e in es:
        den = den + jnp.sum(e, axis=-1, keepdims=True)
    return [(e / den).astype(BF16) for e in es]


_NT = (((1,), (1,)), ((), ()))


def _ctx_attn_kernel(sink_ref, q_ref, k_ref, v_ref, o_ref):
    scale = HD ** -0.5
    for j in range(N_KV):
        kj = k_ref[:, j * HD:(j + 1) * HD].astype(BF16)
        vj = v_ref[:, j * HD:(j + 1) * HD].astype(BF16)
        for g in range(GROUPS):
            hd = j * GROUPS + g
            qh = q_ref[:, hd * HD:(hd + 1) * HD].astype(BF16)
            s = lax.dot_general(qh, kj, _NT, preferred_element_type=F32) * scale
            (p,) = _softmax_parts([s], sink_ref[hd])
            o_ref[:, hd * HD:(hd + 1) * HD] = jnp.dot(p, vj, preferred_element_type=F32)


def _ctx_attn(sink, q, k, v):
    return pl.pallas_call(
        _ctx_attn_kernel,
        grid=(N_CTX_SEQ,),
        in_specs=[
            pl.BlockSpec(memory_space=pltpu.SMEM),
            pl.BlockSpec((SEQ_CTX, Q_DIM), lambda b: (b, 0)),
            pl.BlockSpec((SEQ_CTX, KV_DIM), lambda b: (b, 0)),
            pl.BlockSpec((SEQ_CTX, KV_DIM), lambda b: (b, 0)),
        ],
        out_specs=pl.BlockSpec((SEQ_CTX, Q_DIM), lambda b: (b, 0)),
        out_shape=jax.ShapeDtypeStruct((T_CTX, Q_DIM), F32),
        compiler_params=_cparams(("parallel",)),
        name="ctx_attn",
    )(sink, q, k, v)


def _rope(x, cos, sin_signed):
    n = x.shape[-1]
    lane = lax.broadcasted_iota(jnp.int32, x.shape, 1)
    up = pltpu.roll(x, n - 16, 1)
    dn = pltpu.roll(x, 16, 1)
    partner = jnp.where((lane & 16) == 0, up, dn)
    return x * cos + partner * sin_signed


QB = 128
KWIN = QB + 2 * WINDOW


def _lat_attn_kernel(sink_ref, q_ref, k_ref, v_ref, ck_ref, cv_ref, cq_ref, sq_ref, ckk_ref, skk_ref, o_ref):
    scale = HD ** -0.5
    qb = pl.program_id(1)
    start = pl.multiple_of(jnp.clip(qb * QB - WINDOW, 0, SEQ_LAT - KWIN), WINDOW)
    q = _rope(q_ref[...], cq_ref[...], sq_ref[...])
    kw = _rope(k_ref[pl.ds(start, KWIN), :], ckk_ref[pl.ds(start, KWIN), :], skk_ref[pl.ds(start, KWIN), :])
    vw = v_ref[pl.ds(start, KWIN), :]
    rows = GROUPS * QB
    qpos = qb * QB + lax.broadcasted_iota(jnp.int32, (rows, KWIN), 0) % QB
    kpos = start + lax.broadcasted_iota(jnp.int32, (rows, KWIN), 1)
    valid = jnp.abs(qpos - kpos) <= WINDOW
    for j in range(N_KV):
        sl = slice(j * HD, (j + 1) * HD)
        kj = kw[:, sl].astype(BF16)
        vj = vw[:, sl].astype(BF16)
        ckj = ck_ref[:, sl].astype(BF16)
        cvj = cv_ref[:, sl].astype(BF16)
        heads = [j * GROUPS + g for g in range(GROUPS)]
        qs = jnp.concatenate([q[:, hd * HD:(hd + 1) * HD] for hd in heads], axis=0).astype(BF16)
        sk = jnp.concatenate([jnp.full((QB, 1), sink_ref[hd], F32) for hd in heads], axis=0)
        s_ctx = lax.dot_general(qs, ckj, _NT, preferred_element_type=F32) * scale
        s_loc = lax.dot_general(qs, kj, _NT, preferred_element_type=F32) * scale
        s_loc = jnp.where(valid, s_loc, NEG)
        p_ctx, p_loc = _softmax_parts([s_ctx, s_loc], sk)
        o = jnp.dot(p_ctx, cvj, preferred_element_type=F32) + jnp.dot(p_loc, vj, preferred_element_type=F32)
        for g, hd in enumerate(heads):
            o_ref[:, hd * HD:(hd + 1) * HD] = o[g * QB:(g + 1) * QB, :]


def _rope_tables():
    pos = jnp.arange(SEQ_LAT)
    rows = (pos // GRID_W).astype(F32)
    cols = (pos % GRID_W).astype(F32)
    quarter = HD // 4
    freqs = ROPE_BASE ** (-jnp.arange(quarter, dtype=F32) / quarter)
    ar = rows[:, None] * freqs[None, :]
    ac = cols[:, None] * freqs[None, :]
    cos = jnp.concatenate([jnp.cos(ar), jnp.cos(ar), jnp.cos(ac), jnp.cos(ac)], axis=-1)
    sin = jnp.concatenate([-jnp.sin(ar), jnp.sin(ar), -jnp.sin(ac), jnp.sin(ac)], axis=-1)
    return cos, sin


def _lat_attn(sink, q, k, v, ck, cv):
    cos, sin = _rope_tables()
    cq, sq = jnp.tile(cos, (1, N_HEADS)), jnp.tile(sin, (1, N_HEADS))
    ckk, skk = jnp.tile(cos, (1, N_KV)), jnp.tile(sin, (1, N_KV))
    b0 = T_CTX // SEQ_LAT
    full = lambda shape: pl.BlockSpec(shape, lambda b, i: (0, 0))
    per_b = lambda rows, cols: pl.BlockSpec((None, rows, cols), lambda b, i: (b, 0, 0))
    tok_b = lambda cols: pl.BlockSpec((None, SEQ_LAT, cols), lambda b, i: (b0 + b, 0, 0))
    out = pl.pallas_call(
        _lat_attn_kernel,
        grid=(N_LAT_SEQ, SEQ_LAT // QB),
        in_specs=[
            pl.BlockSpec(memory_space=pltpu.SMEM),
            pl.BlockSpec((None, QB, Q_DIM), lambda b, i: (b0 + b, i, 0)),
            tok_b(KV_DIM),
            tok_b(KV_DIM),
            per_b(ck.shape[1], KV_DIM),
            per_b(cv.shape[1], KV_DIM),
            pl.BlockSpec((QB, Q_DIM), lambda b, i: (i, 0)),
            pl.BlockSpec((QB, Q_DIM), lambda b, i: (i, 0)),
            full((SEQ_LAT, KV_DIM)),
            full((SEQ_LAT, KV_DIM)),
        ],
        out_specs=pl.BlockSpec((None, QB, Q_DIM), lambda b, i: (b, i, 0)),
        out_shape=jax.ShapeDtypeStruct((N_LAT_SEQ, SEQ_LAT, Q_DIM), F32),
        compiler_params=_cparams(("parallel", "parallel")),
        name="lat_attn",
    )(sink, q.reshape(T // SEQ_LAT, SEQ_LAT, Q_DIM), k.reshape(T // SEQ_LAT, SEQ_LAT, KV_DIM),
      v.reshape(T // SEQ_LAT, SEQ_LAT, KV_DIM), ck, cv, cq, sq, ckk, skk)
    return out.reshape(T_LAT, Q_DIM)


ST = 256
N_ST = T // ST


def _seq_tile_info(i):
    n_ctx = T_CTX // ST
    per_seq = SEQ_LAT // ST
    is_ctx = i < n_ctx
    pos0 = jnp.where(is_ctx, 0, ((i - n_ctx) % per_seq) * ST)
    slen = jnp.where(is_ctx, SEQ_CTX, SEQ_LAT)
    return pos0, slen


def _halo_specs(cols):
    return [
        pl.BlockSpec((ST, cols), lambda i: (jnp.maximum(i - 1, 0), 0)),
        pl.BlockSpec((ST, cols), lambda i: (i, 0)),
        pl.BlockSpec((ST, cols), lambda i: (jnp.minimum(i + 1, N_ST - 1), 0)),
    ]


def _with_halo(prev_ref, cur_ref, next_ref, halo, pos0, slen):
    has_prev = (pos0 > 0).astype(F32)
    has_next = (pos0 + ST < slen).astype(F32)
    return jnp.concatenate([prev_ref[ST - halo:, :] * has_prev, cur_ref[...], next_ref[:halo, :] * has_next], axis=0)


def _poolout_kernel(actx_ref, alat_ref, up_ref, uc_ref, un_ref, xc_ref, xl_ref, mod_ref, wp_ref, ps_ref, wo_ref,
                    o_ref):
    i = pl.program_id(0)
    pos0, slen = _seq_tile_info(i)
    halo = 8
    n = ST + 2 * halo
    ext = _with_halo(up_ref, uc_ref, un_ref, halo, pos0, slen)
    t = pos0 + lax.broadcasted_iota(jnp.int32, (ST, 1), 0)
    att = _two_part_tile(actx_ref, alat_ref, ST)
    out = jnp.dot(att.astype(BF16), wo_ref[:Q_DIM, :], preferred_element_type=F32)
    for g, w in enumerate(POOL_SIZES):
        sl = slice(g * PGD, (g + 1) * PGD)
        e = ext[:, sl]
        s = e + pltpu.roll(e, 1, 0)
        step = 1
        while 2 * step < w:
            s = pltpu.roll(s, step, 0) + pltpu.roll(s, n - step, 0)
            step *= 2
        lo = jnp.maximum(t - w // 2, 0)
        hi = jnp.minimum(t + (w - w // 2), slen)
        mean = s[halo:halo + ST, :] / (hi - lo).astype(F32)
        pooled = mean - e[halo:halo + ST, :]
        mixed = jnp.dot(pooled.astype(BF16), wp_ref[g], preferred_element_type=F32) * ps_ref[:, sl]
        out = out + jnp.dot(mixed.astype(BF16), wo_ref[Q_DIM + g * PGD:Q_DIM + (g + 1) * PGD, :],
                            preferred_element_type=F32)
    o_ref[...] = _two_part_tile(xc_ref, xl_ref, ST) + mod_ref[2:3, :] * out


def _poolout(att_ctx, att_lat, u, x_ctx, x_lat, mods, w_pool_b, pool_scale, w_out_b):
    return pl.pallas_call(
        _poolout_kernel,
        grid=(N_ST,),
        in_specs=_two_part_specs(ST, Q_DIM) + _halo_specs(POOL_DIM) + _two_part_specs(ST, D) + [
            pl.BlockSpec((None, 6, D), lambda i: (_group_of_tile(i, ST), 0, 0)),
            pl.BlockSpec((len(POOL_SIZES), PGD, PGD), lambda i: (0, 0, 0)),
            pl.BlockSpec((1, POOL_DIM), lambda i: (0, 0)),
            pl.BlockSpec((Q_DIM + POOL_DIM, D), lambda i: (0, 0)),
        ],
        out_specs=pl.BlockSpec((ST, D), lambda i: (i, 0)),
        out_shape=jax.ShapeDtypeStruct((T, D), F32),
        compiler_params=_cparams(("parallel",)),
        name="poolout",
    )(att_ctx, att_lat, u, u, u, x_ctx, x_lat, mods, w_pool_b, pool_scale, w_out_b)


def _glu_kernel(x_ref, g_ref, mod_ref, w_ref, b_ref, o_ref):
    h = _modnorm(x_ref[...], g_ref[...], mod_ref[0:1, :], mod_ref[1:2, :])
    a = jnp.dot(h.astype(BF16), w_ref[...], preferred_element_type=F32) + b_ref[...]
    o_ref[...] = a[:, :D] * jax.nn.sigmoid(a[:, D:])


def _glu(x, g, mods, w1_b, b1):
    tt = 512
    return pl.pallas_call(
        _glu_kernel,
        grid=(T // tt,),
        in_specs=[
            pl.BlockSpec((tt, D), lambda i: (i, 0)),
            pl.BlockSpec((1, D), lambda i: (0, 0)),
            pl.BlockSpec((None, 6, D), lambda i: (_group_of_tile(i, tt), 0, 0)),
            pl.BlockSpec((D, 2 * D), lambda i: (0, 0)),
            pl.BlockSpec((1, 2 * D), lambda i: (0, 0)),
        ],
        out_specs=pl.BlockSpec((tt, D), lambda i: (i, 0)),
        out_shape=jax.ShapeDtypeStruct((T, D), F32),
        compiler_params=_cparams(("parallel",)),
        name="glu",
    )(x, g, mods, w1_b, b1)


CONV_HALO = 16
SUBLANES = 8


def _convout_kernel(ap_ref, ac_ref, an_ref, x_ref, mod_ref, dw_ref, dwb_ref, lg_ref, lb_ref, w2_ref, b2_ref,
                    o_ref, ext_ref, sh_ref):
    i = pl.program_id(0)
    pos0, slen = _seq_tile_info(i)
    ext_ref[...] = _with_halo(ap_ref, ac_ref, an_ref, CONV_HALO, pos0, slen)
    n_sh = ST + 2 * CONV_HALO - SUBLANES
    for r in range(1, SUBLANES):
        sh_ref[r - 1] = ext_ref[pl.ds(r, n_sh), :]
    pad = CONV_W // 2
    half = ST // 2
    parts = []
    for c0 in range(0, D, 128):
        cs = slice(c0, c0 + 128)
        halves = []
        for r0 in (0, half):
            acc = jnp.zeros((half, 128), F32)
            for k in range(CONV_W):
                off = CONV_HALO - pad + k
                r = off % SUBLANES
                src = ext_ref if r == 0 else sh_ref.at[r - 1]
                acc = acc + src[pl.ds(off - r + r0, half), cs] * dw_ref[k:k + 1, cs]
            halves.append(acc)
        parts.append(jnp.concatenate(halves, axis=0))
    a = jnp.concatenate(parts, axis=1) + dwb_ref[...]
    mu = jnp.mean(a, axis=-1, keepdims=True)
    var = jnp.mean(jnp.square(a - mu), axis=-1, keepdims=True)
    y = (a - mu) * lax.rsqrt(var + EPS) * lg_ref[...] + lb_ref[...]
    y = jax.nn.silu(y)
    out = jnp.dot(y.astype(BF16), w2_ref[...], preferred_element_type=F32) + b2_ref[...]
    o_ref[...] = x_ref[...] + mod_ref[2:3, :] * out


def _convout(a, x, mods, dw, dwb, lg, lb, w2_b, b2):
    row = lambda: pl.BlockSpec((1, D), lambda i: (0, 0))
    return pl.pallas_call(
        _convout_kernel,
        grid=(N_ST,),
        in_specs=_halo_specs(D) + [
            pl.BlockSpec((ST, D), lambda i: (i, 0)),
            pl.BlockSpec((None, 6, D), lambda i: (_group_of_tile(i, ST), 0, 0)),
            pl.BlockSpec((CONV_W, D), lambda i: (0, 0)),
            row(), row(), row(),
            pl.BlockSpec((D, D), lambda i: (0, 0)),
            row(),
        ],
        out_specs=pl.BlockSpec((ST, D), lambda i: (i, 0)),
        out_shape=jax.ShapeDtypeStruct((T, D), F32),
        scratch_shapes=[pltpu.VMEM((ST + 2 * CONV_HALO, D), F32),
                        pltpu.VMEM((SUBLANES - 1, ST + 2 * CONV_HALO - SUBLANES, D), F32)],
        compiler_params=_cparams(("parallel",)),
        name="convout",
    )(a, a, a, x, mods, dw, dwb, lg, lb, w2_b, b2)


SEL_TT = 256
LANES = 128


def _topk16_exact(problems):
    vals = [[] for _ in problems]
    for r in range(P_TOPK):
        for i, (work_ref, rank_ref) in enumerate(problems):
            shape = work_ref.shape
            iota = lax.broadcasted_iota(jnp.int32, shape, 0).astype(F32)
            s = work_ref[...]
            m = jnp.max(s, axis=0, keepdims=True)
            hit = iota == jnp.min(jnp.where(s == m, iota, float(shape[0])), axis=0, keepdims=True)
            pltpu.store(work_ref, jnp.full(shape, -INF, F32), mask=hit)
            pltpu.store(rank_ref, jnp.full(shape, float(r), F32), mask=hit)
            vals[i].append(m)
    return [jnp.concatenate(v, axis=0) for v in vals]


def _compare_exchange(t, hi, lo):
    if t[lo] is None:
        return
    if t[hi] is None:
        t[hi], t[lo] = t[lo], None
        return
    t[hi], t[lo] = jnp.maximum(t[hi], t[lo]), jnp.minimum(t[hi], t[lo])


def _top16_sorted(scores):
    n = P_TOPK
    n_tiles = scores.shape[0] // SUBLANES
    t = [scores[SUBLANES * i:SUBLANES * (i + 1), :] if i < n_tiles else None for i in range(n)]
    k = 2
    while k <= n:
        j = k // 2
        while j >= 1:
            for i in range(n):
                m = i ^ j
                if m > i:
                    if i & k == 0:
                        _compare_exchange(t, i, m)
                    else:
                        _compare_exchange(t, m, i)
            j //= 2
        k *= 2
    for shift in (4, 2, 1):
        other = [None if x is None else pltpu.roll(x, shift, 0) for x in t]

        def larger(a, b):
            return b if a is None else a if b is None else jnp.maximum(a, b)

        t = [larger(t[i], other[n - 1 - i]) for i in range(n)]
        j = n // 2
        while j >= 1:
            for i in range(n):
                if i ^ j > i:
                    _compare_exchange(t, i, i ^ j)
            j //= 2
    return jnp.concatenate([x[0:1, :] for x in t], axis=0)


def _count_ge(s, thr):
    return jnp.sum((s >= thr).astype(F32), axis=0, keepdims=True)


_CAND_BLOCKS = [(0, 0, 8), (0, 8, 8)] + [(r1, 0, P_TOPK // (r1 + 1)) for r1 in range(1, 8)] + [(None, 0, 8)]


N_CAND = 8 * len(_CAND_BLOCKS)
HEADS_PER_TRIP = 4


def _select_kernel(x_ref, g_ref, mod_ref, wq_ref, keys_ref, hb_ref, r_ref, e2_ref, c_ref, e1_ref,
                   q_scr, s_scr, wk_scr, rk_scr, cw_scr, cr_scr, ce_scr):
    h = _modnorm(x_ref[...], g_ref[...], mod_ref[3:4, :], mod_ref[4:5, :])
    hb = h.astype(BF16)
    hb_ref[...] = h.T.astype(BF16).astype(F32)
    q_scr[...] = jnp.dot(hb, wq_ref[...], preferred_element_type=F32).astype(BF16)
    iota8 = lax.broadcasted_iota(jnp.int32, (8, LANES), 0)

    n_lc = SEL_TT // LANES

    def head_group(i, carry):
        for j in range(HEADS_PER_TRIP):
            hd = i * HEADS_PER_TRIP + j
            for p in range(2):
                col = pl.multiple_of((hd * 2 + p) * N_KEYS, N_KEYS)
                s_scr[j, p] = lax.dot_general(keys_ref[hd * 2 + p], q_scr[:, pl.ds(col, N_KEYS)], _NT,
                                              preferred_element_type=F32)
        tied = select_group(i, exact=False)

        @pl.when(jnp.max(tied) > 0.0)
        def _():
            select_group(i, exact=True)

        return carry

    def select_group(i, exact):
        probs = [(i * HEADS_PER_TRIP + j, j, lc, j * n_lc + lc) for j in range(HEADS_PER_TRIP) for lc in range(n_lc)]
        tied = jnp.zeros((1, LANES), F32)
        for hd, j, lc, sc in probs:
            ls = slice(lc * LANES, (lc + 1) * LANES)
            for p in range(2):
                wk_scr[sc, p] = s_scr[j, p, :, ls]
                rk_scr[sc, p] = jnp.full((N_KEYS, LANES), float(P_TOPK), F32)
        if exact:
            tops = _topk16_exact([(wk_scr.at[sc, p], rk_scr.at[sc, p]) for _, _, _, sc in probs for p in range(2)])
        else:
            tops = [_top16_sorted(wk_scr[sc, p]) for _, _, _, sc in probs for p in range(2)]
        for n, (hd, j, lc, sc) in enumerate(probs):
            v1, v2 = tops[2 * n], tops[2 * n + 1]
            for b, (r1, r2, nvalid) in enumerate(_CAND_BLOCKS):
                if r1 is None:
                    blk = v1[8:16, :] + v2[0:1, :]
                else:
                    blk = v1[r1:r1 + 1, :] + v2[r2:r2 + 8, :]
                if nvalid < 8:
                    blk = jnp.where(iota8 < nvalid, blk, -INF)
                cw_scr[sc, 8 * b:8 * b + 8, :] = blk
                ce_scr[sc, 8 * b:8 * b + 8, :] = jnp.exp(blk - (v1[0:1, :] + v2[0:1, :]))
            if exact:
                cr_scr[sc] = jnp.full((N_CAND, LANES), float(P_TOPK), F32)
        if exact:
            _topk16_exact([(cw_scr.at[sc], cr_scr.at[sc]) for _, _, _, sc in probs])
        else:
            ctops = [_top16_sorted(cw_scr[sc]) for _, _, _, sc in probs]
        for n, (hd, j, lc, sc) in enumerate(probs):
            ls = slice(lc * LANES, (lc + 1) * LANES)
            v1, v2 = tops[2 * n], tops[2 * n + 1]
            s1 = s_scr[j, 0, :, ls]
            s2 = s_scr[j, 1, :, ls]
            if exact:
                rank1 = rk_scr[sc, 0]
                sel = cr_scr[sc] < float(P_TOPK)
            else:
                sel = cw_scr[sc] >= ctops[n][P_TOPK - 1:P_TOPK, :]
                for cnt_ge in (_count_ge(s1, v1[P_TOPK - 1:P_TOPK, :]), _count_ge(s2, v2[P_TOPK - 1:P_TOPK, :]),
                               jnp.sum(sel.astype(F32), axis=0, keepdims=True)):
                    tied = jnp.maximum(tied, (cnt_ge != float(P_TOPK)).astype(F32))
                for v in (v1, v2):
                    repeats = (v[:P_TOPK - 1, :] == v[1:, :]).astype(F32)
                    tied = jnp.maximum(tied, jnp.max(repeats, axis=0, keepdims=True))
            self32 = sel.astype(F32)
            z = jnp.sum(jnp.where(sel, ce_scr[sc], 0.0), axis=0, keepdims=True)
            cnt_rows = [jnp.sum(self32[0:16, :], axis=0, keepdims=True)]
            for b in range(2, 9):
                cnt_rows.append(jnp.sum(self32[8 * b:8 * b + 8, :], axis=0, keepdims=True))
            cnt = jnp.concatenate(cnt_rows + [self32[72:80, :]], axis=0)
            if exact:
                c_ref[hd, lc] = jnp.zeros((N_KEYS, LANES), F32)
                for r in range(P_TOPK):
                    pltpu.store(c_ref.at[hd, lc], jnp.broadcast_to(cnt[r:r + 1, :], (N_KEYS, LANES)),
                                mask=rank1 == float(r))
                rank2 = rk_scr[sc, 1]
            else:
                kept = jnp.zeros((N_KEYS, LANES), F32)
                rank2 = jnp.full((N_KEYS, LANES), float(P_TOPK), F32)
                for r in range(P_TOPK):
                    kept = jnp.where(s1 == v1[r:r + 1, :], cnt[r:r + 1, :], kept)
                    rank2 = jnp.where(s2 == v2[r:r + 1, :], float(r), rank2)
                c_ref[hd, lc] = kept
            r_ref[hd, :, ls] = rank2.astype(BF16)
            e2_ref[hd, :, ls] = jnp.exp(s2 - v2[0:1, :]).astype(BF16)
            e1_ref[hd, lc] = jnp.exp(s1 - v1[0:1, :]) * (0.5 / z)
        return tied

    lax.fori_loop(0, P_HEADS // HEADS_PER_TRIP, head_group, 0)


def _select(x, g, mods, wq_b, keys_b):
    tt = SEL_TT
    hk = pl.BlockSpec((P_HEADS, N_KEYS, tt), lambda i: (0, 0, i))
    hs = pl.BlockSpec((P_HEADS, tt // LANES, N_KEYS, LANES), lambda i: (0, i, 0, 0))
    n_slot = HEADS_PER_TRIP * (tt // LANES)
    return pl.pallas_call(
        _select_kernel,
        grid=(T // tt,),
        in_specs=[
            pl.BlockSpec((tt, D), lambda i: (i, 0)),
            pl.BlockSpec((1, D), lambda i: (0, 0)),
            pl.BlockSpec((None, 6, D), lambda i: (_group_of_tile(i, tt), 0, 0)),
            pl.BlockSpec((D, 2 * P_HEADS * N_KEYS), lambda i: (0, 0)),
            pl.BlockSpec((2 * P_HEADS, N_KEYS, N_KEYS), lambda i: (0, 0, 0)),
        ],
        out_specs=[pl.BlockSpec((D, tt), lambda i: (0, i)), hk, hk, hs, hs],
        out_shape=[
            jax.ShapeDtypeStruct((D, T), F32),
            jax.ShapeDtypeStruct((P_HEADS, N_KEYS, T), BF16),
            jax.ShapeDtypeStruct((P_HEADS, N_KEYS, T), BF16),
            jax.ShapeDtypeStruct((P_HEADS, T // LANES, N_KEYS, LANES), F32),
            jax.ShapeDtypeStruct((P_HEADS, T // LANES, N_KEYS, LANES), F32),
        ],
        scratch_shapes=[
            pltpu.VMEM((tt, 2 * P_HEADS * N_KEYS), BF16),
            pltpu.VMEM((HEADS_PER_TRIP, 2, N_KEYS, tt), F32),
            pltpu.VMEM((n_slot, 2, N_KEYS, LANES), F32),
            pltpu.VMEM((n_slot, 2, N_KEYS, LANES), F32),
            pltpu.VMEM((n_slot, N_CAND, LANES), F32),
            pltpu.VMEM((n_slot, N_CAND, LANES), F32),
            pltpu.VMEM((n_slot, N_CAND, LANES), F32),
        ],
        compiler_params=_cparams(("parallel",)),
        name="peer_select",
    )(x, g, mods, wq_b, keys_b)


DT = 512
DE = 2048
I1_PER = DE // N_KEYS
PACK = 16
DSUB = 256


def _row_replicated(ref, hd, sub, ii):
    slabs = [jnp.broadcast_to(ref[hd, sub * (DSUB // LANES) + s, ii:ii + 1, :], (PACK, LANES))
             for s in range(DSUB // LANES)]
    return jnp.concatenate(slabs, axis=1)


def _dense_kernel(final_norm, hb_ref, u_ref, vt_ref, r_ref, e2_ref, c_ref, e1_ref, x_ref, mod_ref, fg_ref, *refs):
    *out_refs, acc_ref, p_ref = refs
    k = pl.program_id(1)

    @pl.when(k == 0)
    def _():
        acc_ref[...] = jnp.zeros_like(acc_ref)

    def scores(sub):
        return jnp.dot(u_ref[...], hb_ref[:, sub * DSUB:(sub + 1) * DSUB], preferred_element_type=F32)

    n_sub = DT // DSUB
    a_next = scores(0)
    for sub in range(n_sub):
        ts = slice(sub * DSUB, (sub + 1) * DSUB)
        a_t = a_next
        if sub + 1 < n_sub:
            a_next = scores(sub + 1)
        act = (a_t * (1.0 + lax.erf(a_t * math.sqrt(0.5)))).astype(BF16)
        for ii in range(I1_PER):
            w = jnp.zeros((N_KEYS // PACK, PACK, DSUB), BF16)
            for hd in range(P_HEADS):
                cb = _row_replicated(c_ref, hd, sub, ii).astype(BF16)
                eb = _row_replicated(e1_ref, hd, sub, ii).astype(BF16)
                w = w + jnp.where(r_ref[hd, :, :, ts] < cb[None], e2_ref[hd, :, :, ts], jnp.zeros((), BF16)) * eb[None]
            rows = slice(ii * N_KEYS, (ii + 1) * N_KEYS)
            p_ref[rows, ts] = act[rows, :] * w.reshape(N_KEYS, DSUB)
        acc_ref[:, ts] += jnp.dot(vt_ref[...], p_ref[:, ts], preferred_element_type=F32)

    @pl.when(k == pl.num_programs(1) - 1)
    def _():
        y = x_ref[...] + mod_ref[5:6, :] * acc_ref[...].T
        if not final_norm:
            out_refs[0][...] = y
        else:
            y = y * lax.rsqrt(jnp.mean(y * y, axis=-1, keepdims=True) + EPS) * fg_ref[...]
            is_ctx = pl.program_id(0) < T_CTX // DT

            @pl.when(is_ctx)
            def _():
                out_refs[0][...] = y

            @pl.when(jnp.logical_not(is_ctx))
            def _():
                out_refs[1][...] = y


def _dense(layer, hb, u_b, vt_b, rk, e2, c, e1, x, mods, fg, final_norm):
    hk_b = pl.BlockSpec((P_HEADS, N_KEYS // PACK, PACK, DT), lambda j, k: (0, 0, 0, j))
    hk_f = pl.BlockSpec((P_HEADS, DT // LANES, I1_PER, LANES), lambda j, k: (0, j, k, 0))
    rk4 = rk.reshape(P_HEADS, N_KEYS // PACK, PACK, T)
    e24 = e2.reshape(P_HEADS, N_KEYS // PACK, PACK, T)
    if final_norm:
        n_ctx = T_CTX // DT
        out_specs = [pl.BlockSpec((DT, D), lambda j, k: (jnp.minimum(j, n_ctx - 1), 0)),
                     pl.BlockSpec((DT, D), lambda j, k: (jnp.maximum(j - n_ctx, 0), 0))]
        out_shape = [jax.ShapeDtypeStruct((T_CTX, D), F32), jax.ShapeDtypeStruct((T_LAT, D), F32)]
    else:
        out_specs = pl.BlockSpec((DT, D), lambda j, k: (j, 0))
        out_shape = jax.ShapeDtypeStruct((T, D), F32)
    return pl.pallas_call(
        functools.partial(_dense_kernel, final_norm),
        grid=(T // DT, N_EXPERTS // DE),
        in_specs=[
            pl.BlockSpec((D, DT), lambda j, k: (0, j)),
            pl.BlockSpec((None, DE, D), lambda j, k: (layer, k, 0)),
            pl.BlockSpec((None, D, DE), lambda j, k: (layer, 0, k)),
            hk_b, hk_b, hk_f, hk_f,
            pl.BlockSpec((DT, D), lambda j, k: (j, 0)),
            pl.BlockSpec((None, 6, D), lambda j, k: (_group_of_tile(j, DT), 0, 0)),
            pl.BlockSpec((1, D), lambda j, k: (0, 0)),
        ],
        out_specs=out_specs,
        out_shape=out_shape,
        scratch_shapes=[pltpu.VMEM((D, DT), F32), pltpu.VMEM((DE, DT), BF16)],
        compiler_params=_cparams(("arbitrary", "arbitrary")),
        name="peer_dense",
    )(hb, u_b, vt_b, rk4, e24, c, e1, x, mods, fg)


def _transpose_cast_kernel(v_ref, o_ref):
    o_ref[...] = v_ref[...].T.astype(BF16)


def _value_tables(peer_v):
    n_layers = peer_v.shape[0]
    blk = 1024
    return pl.pallas_call(
        _transpose_cast_kernel,
        grid=(n_layers, N_EXPERTS // blk),
        in_specs=[pl.BlockSpec((None, blk, D), lambda l, e: (l, e, 0))],
        out_specs=pl.BlockSpec((None, D, blk), lambda l, e: (l, 0, e)),
        out_shape=jax.ShapeDtypeStruct((n_layers, D, N_EXPERTS), BF16),
        compiler_params=_cparams(("parallel", "parallel")),
        name="value_tables",
    )(peer_v)


def _peer(layer, x, g, mods, wq, keys, u_b, vt_b, fg, final_norm):
    hb, rk, e2, c, e1 = _select(x, g, mods, wq.astype(BF16),
                                keys.reshape(2 * P_HEADS, N_KEYS, N_KEYS).astype(BF16))
    return _dense(layer, hb, u_b, vt_b, rk, e2, c, e1, x, mods, fg, final_norm)


def kernel(x_prompt, x_sample, c, cache_k, cache_v, c_ctx, mod_w, mod_b, norm_mix_g, norm_ffn_g, w_in, attn_sink, w_pool, pool_scale, w_out, conv_w1, conv_b1, conv_dw, conv_dw_b, conv_ln_g, conv_ln_b, conv_w2, conv_b2, peer_wq, peer_keys, peer_u, peer_v, final_norm_g):
    x_ctx, x_lat = x_prompt.reshape(T_CTX, D), x_sample.reshape(T_LAT, D)
    cvec = jnp.concatenate([c_ctx[None, :], c, jnp.zeros((N_GROUPS - 1 - N_LAT_SEQ, D), F32)], axis=0)
    mods_all = _adaln(cvec, mod_w, mod_b).reshape(mod_w.shape[0], N_GROUPS, 6, D)
    row = lambda a: a.reshape(1, -1)
    fg = row(final_norm_g)

    mods = mods_all[0]
    q, k, v, u = _inproj(x_ctx, x_lat, row(norm_mix_g[0]), mods, w_in[0].astype(BF16))
    att_ctx = _ctx_attn(attn_sink[0], q, k, v)
    n_past = cache_k.shape[2]
    att_lat = _lat_attn(attn_sink[0], q, k, v,
                        cache_k[:, 0].reshape(N_LAT_SEQ, n_past, KV_DIM),
                        cache_v[:, 0].reshape(N_LAT_SEQ, n_past, KV_DIM))
    x = _poolout(att_ctx, att_lat, u, x_ctx, x_lat, mods, w_pool[0].astype(BF16), row(pool_scale[0]), w_out[0].astype(BF16))
    u_b = peer_u
    vt_b = _value_tables(peer_v)
    x = _peer(0, x, row(norm_ffn_g[0]), mods, peer_wq[0], peer_keys[0], u_b, vt_b, fg, False)
    state_k = k[:T_CTX].reshape(N_CTX_SEQ, 1, SEQ_CTX, N_KV, HD)
    state_v = v[:T_CTX].reshape(N_CTX_SEQ, 1, SEQ_CTX, N_KV, HD)

    mods = mods_all[1]
    a = _glu(x, row(norm_mix_g[1]), mods, conv_w1[0].astype(BF16), row(conv_b1[0]))
    x = _convout(a, x, mods, conv_dw[0], row(conv_dw_b[0]), row(conv_ln_g[0]), row(conv_ln_b[0]),
                 conv_w2[0].astype(BF16), row(conv_b2[0]))
    y_ctx, y_lat = _peer(1, x, row(norm_ffn_g[1]), mods, peer_wq[1], peer_keys[1], u_b, vt_b, fg, True)

    y_prompt = y_ctx.reshape(N_CTX_SEQ, SEQ_CTX, D)
    y_sample = y_lat.reshape(N_LAT_SEQ, SEQ_LAT, D)
    return (y_prompt, y_sample, state_k, state_v)
```

```python
import functools
import math

import jax
import jax.numpy as jnp
from jax import lax
from jax.experimental import pallas as pl
from jax.experimental.pallas import tpu as pltpu

F32 = jnp.float32
BF16 = jnp.bfloat16

D = 1024
N_CTX_SEQ = 16
SEQ_CTX = 256
N_LAT_SEQ = 2
SEQ_LAT = 2048
T_CTX = N_CTX_SEQ * SEQ_CTX
T_LAT = N_LAT_SEQ * SEQ_LAT
T = T_CTX + T_LAT
N_GROUPS = 8
GRID_W = 64
ROPE_BASE = 10000.0
N_HEADS = 8
N_KV = 2
GROUPS = N_HEADS // N_KV
HD = 64
WINDOW = 128
Q_DIM = N_HEADS * HD
KV_DIM = N_KV * HD
POOL_SIZES = (2, 4, 8, 16)
POOL_DIM = 512
PGD = 128
IN_DIM = Q_DIM + 2 * KV_DIM + POOL_DIM
CONV_W = 31
N_KEYS = 128
N_EXPERTS = N_KEYS * N_KEYS
P_HEADS = 8
P_TOPK = 16
EPS = 1e-6
NEG = -1e30
INF = float("inf")

VMEM_LIMIT = 56 * 1024 * 1024


def _cparams(sem, flags=None):
    return pltpu.CompilerParams(dimension_semantics=sem, vmem_limit_bytes=VMEM_LIMIT, flags=flags)


def _group_of_tile(i, tile):
    n_ctx = T_CTX // tile
    per_seq = SEQ_LAT // tile
    return jnp.where(i < n_ctx, 0, 1 + (i - n_ctx) // per_seq)


def _modnorm(x, g, shift, scale):
    y = x * lax.rsqrt(jnp.mean(x * x, axis=-1, keepdims=True) + EPS)
    y = y * g
    return y * (1 + scale) + shift


def _adaln_kernel(cv_ref, w_ref, b_ref, o_ref):
    a = jax.nn.silu(cv_ref[...]).astype(BF16)
    o_ref[...] = jnp.dot(a, w_ref[...].astype(BF16), preferred_element_type=F32) + b_ref[...]


def _adaln(cvec, mod_w, mod_b):
    L, _, n6 = mod_w.shape
    tn = 1536
    return pl.pallas_call(
        _adaln_kernel,
        grid=(L, n6 // tn),
        in_specs=[
            pl.BlockSpec((N_GROUPS, D), lambda l, n: (0, 0)),
            pl.BlockSpec((None, D, tn), lambda l, n: (l, 0, n)),
            pl.BlockSpec((None, 1, tn), lambda l, n: (l, 0, n)),
        ],
        out_specs=pl.BlockSpec((None, N_GROUPS, tn), lambda l, n: (l, 0, n)),
        out_shape=jax.ShapeDtypeStruct((L, N_GROUPS, n6), F32),
        compiler_params=_cparams(("parallel", "parallel")),
        name="adaln",
    )(cvec, mod_w, mod_b.reshape(L, 1, n6))


def _two_part_specs(tile, cols):
    n_ctx = T_CTX // tile
    return [pl.BlockSpec((tile, cols), lambda i: (jnp.minimum(i, n_ctx - 1), 0)),
            pl.BlockSpec((tile, cols), lambda i: (jnp.maximum(i - n_ctx, 0), 0))]


def _two_part_tile(ctx_ref, lat_ref, tile):
    return jnp.where(pl.program_id(0) < T_CTX // tile, ctx_ref[...], lat_ref[...])


def _inproj_kernel(xc_ref, xl_ref, g_ref, mod_ref, w_ref, q_ref, k_ref, v_ref, u_ref):
    h = _modnorm(_two_part_tile(xc_ref, xl_ref, INPROJ_TT), g_ref[...], mod_ref[0:1, :], mod_ref[1:2, :])
    p = jnp.dot(h.astype(BF16), w_ref[...], preferred_element_type=F32)
    q_ref[...] = p[:, :Q_DIM]
    k_ref[...] = p[:, Q_DIM:Q_DIM + KV_DIM]
    v_ref[...] = p[:, Q_DIM + KV_DIM:Q_DIM + 2 * KV_DIM]
    u_ref[...] = p[:, Q_DIM + 2 * KV_DIM:]


INPROJ_TT = 512


def _inproj(x_ctx, x_lat, g, mods, w_in_b):
    tt = INPROJ_TT
    return pl.pallas_call(
        _inproj_kernel,
        grid=(T // tt,),
        in_specs=_two_part_specs(tt, D) + [
            pl.BlockSpec((1, D), lambda i: (0, 0)),
            pl.BlockSpec((None, 6, D), lambda i: (_group_of_tile(i, tt), 0, 0)),
            pl.BlockSpec((D, IN_DIM), lambda i: (0, 0)),
        ],
        out_specs=[
            pl.BlockSpec((tt, Q_DIM), lambda i: (i, 0)),
            pl.BlockSpec((tt, KV_DIM), lambda i: (i, 0)),
            pl.BlockSpec((tt, KV_DIM), lambda i: (i, 0)),
            pl.BlockSpec((tt, POOL_DIM), lambda i: (i, 0)),
        ],
        out_shape=[
            jax.ShapeDtypeStruct((T, Q_DIM), F32),
            jax.ShapeDtypeStruct((T, KV_DIM), F32),
            jax.ShapeDtypeStruct((T, KV_DIM), F32),
            jax.ShapeDtypeStruct((T, POOL_DIM), F32),
        ],
        compiler_params=_cparams(("parallel",)),
        name="inproj",
    )(x_ctx, x_lat, g, mods, w_in_b)


def _softmax_parts(parts, sk):
    m = sk
    for s in parts:
        m = jnp.maximum(m, jnp.max(s, axis=-1, keepdims=True))
    es = [jnp.exp(s - m) for s in parts]
    den = jnp.exp(sk - m)
    for e in es:
        den = den + jnp.sum(e, axis=-1, keepdims=True)
    return [(e / den).astype(BF16) for e in es]


_NT = (((1,), (1,)), ((), ()))


def _ctx_attn_kernel(sink_ref, q_ref, k_ref, v_ref, o_ref):
    scale = HD ** -0.5
    for j in range(N_KV):
        kj = k_ref[:, j * HD:(j + 1) * HD].astype(BF16)
        vj = v_ref[:, j * HD:(j + 1) * HD].astype(BF16)
        for g in range(GROUPS):
            hd = j * GROUPS + g
            qh = q_ref[:, hd * HD:(hd + 1) * HD].astype(BF16)
            s = lax.dot_general(qh, kj, _NT, preferred_element_type=F32) * scale
            (p,) = _softmax_parts([s], sink_ref[hd])
            o_ref[:, hd * HD:(hd + 1) * HD] = jnp.dot(p, vj, preferred_element_type=F32)


def _ctx_attn(sink, q, k, v):
    return pl.pallas_call(
        _ctx_attn_kernel,
        grid=(N_CTX_SEQ,),
        in_specs=[
            pl.BlockSpec(memory_space=pltpu.SMEM),
            pl.BlockSpec((SEQ_CTX, Q_DIM), lambda b: (b, 0)),
            pl.BlockSpec((SEQ_CTX, KV_DIM), lambda b: (b, 0)),
            pl.BlockSpec((SEQ_CTX, KV_DIM), lambda b: (b, 0)),
        ],
        out_specs=pl.BlockSpec((SEQ_CTX, Q_DIM), lambda b: (b, 0)),
        out_shape=jax.ShapeDtypeStruct((T_CTX, Q_DIM), F32),
        compiler_params=_cparams(("parallel",)),
        name="ctx_attn",
    )(sink, q, k, v)


def _rope(x, cos, sin_signed):
    n = x.shape[-1]
    lane = lax.broadcasted_iota(jnp.int32, x.shape, 1)
    up = pltpu.roll(x, n - 16, 1)
    dn = pltpu.roll(x, 16, 1)
    partner = jnp.where((lane & 16) == 0, up, dn)
    return x * cos + partner * sin_signed


QB = 128
KWIN = QB + 2 * WINDOW


def _lat_attn_kernel(sink_ref, q_ref, k_ref, v_ref, ck_ref, cv_ref, cq_ref, sq_ref, ckk_ref, skk_ref, o_ref):
    scale = HD ** -0.5
    qb = pl.program_id(1)
    start = pl.multiple_of(jnp.clip(qb * QB - WINDOW, 0, SEQ_LAT - KWIN), WINDOW)
    q = _rope(q_ref[...], cq_ref[...], sq_ref[...])
    kw = _rope(k_ref[pl.ds(start, KWIN), :], ckk_ref[pl.ds(start, KWIN), :], skk_ref[pl.ds(start, KWIN), :])
    vw = v_ref[pl.ds(start, KWIN), :]
    rows = GROUPS * QB
    qpos = qb * QB + lax.broadcasted_iota(jnp.int32, (rows, KWIN), 0) % QB
    kpos = start + lax.broadcasted_iota(jnp.int32, (rows, KWIN), 1)
    valid = jnp.abs(qpos - kpos) <= WINDOW
    for j in range(N_KV):
        sl = slice(j * HD, (j + 1) * HD)
        kj = kw[:, sl].astype(BF16)
        vj = vw[:, sl].astype(BF16)
        ckj = ck_ref[:, sl].astype(BF16)
        cvj = cv_ref[:, sl].astype(BF16)
        heads = [j * GROUPS + g for g in range(GROUPS)]
        qs = jnp.concatenate([q[:, hd * HD:(hd + 1) * HD] for hd in heads], axis=0).astype(BF16)
        sk = jnp.concatenate([jnp.full((QB, 1), sink_ref[hd], F32) for hd in heads], axis=0)
        s_ctx = lax.dot_general(qs, ckj, _NT, preferred_element_type=F32) * scale
        s_loc = lax.dot_general(qs, kj, _NT, preferred_element_type=F32) * scale
        s_loc = jnp.where(valid, s_loc, NEG)
        p_ctx, p_loc = _softmax_parts([s_ctx, s_loc], sk)
        o = jnp.dot(p_ctx, cvj, preferred_element_type=F32) + jnp.dot(p_loc, vj, preferred_element_type=F32)
        for g, hd in enumerate(heads):
            o_ref[:, hd * HD:(hd + 1) * HD] = o[g * QB:(g + 1) * QB, :]


def _rope_tables():
    pos = jnp.arange(SEQ_LAT)
    rows = (pos // GRID_W).astype(F32)
    cols = (pos % GRID_W).astype(F32)
    quarter = HD // 4
    freqs = ROPE_BASE ** (-jnp.arange(quarter, dtype=F32) / quarter)
    ar = rows[:, None] * freqs[None, :]
    ac = cols[:, None] * freqs[None, :]
    cos = jnp.concatenate([jnp.cos(ar), jnp.cos(ar), jnp.cos(ac), jnp.cos(ac)], axis=-1)
    sin = jnp.concatenate([-jnp.sin(ar), jnp.sin(ar), -jnp.sin(ac), jnp.sin(ac)], axis=-1)
    return cos, sin


def _lat_attn(sink, q, k, v, ck, cv):
    cos, sin = _rope_tables()
    cq, sq = jnp.tile(cos, (1, N_HEADS)), jnp.tile(sin, (1, N_HEADS))
    ckk, skk = jnp.tile(cos, (1, N_KV)), jnp.tile(sin, (1, N_KV))
    b0 = T_CTX // SEQ_LAT
    full = lambda shape: pl.BlockSpec(shape, lambda b, i: (0, 0))
    per_b = lambda rows, cols: pl.BlockSpec((None, rows, cols), lambda b, i: (b, 0, 0))
    tok_b = lambda cols: pl.BlockSpec((None, SEQ_LAT, cols), lambda b, i: (b0 + b, 0, 0))
    out = pl.pallas_call(
        _lat_attn_kernel,
        grid=(N_LAT_SEQ, SEQ_LAT // QB),
        in_specs=[
            pl.BlockSpec(memory_space=pltpu.SMEM),
            pl.BlockSpec((None, QB, Q_DIM), lambda b, i: (b0 + b, i, 0)),
            tok_b(KV_DIM),
            tok_b(KV_DIM),
            per_b(ck.shape[1], KV_DIM),
            per_b(cv.shape[1], KV_DIM),
            pl.BlockSpec((QB, Q_DIM), lambda b, i: (i, 0)),
            pl.BlockSpec((QB, Q_DIM), lambda b, i: (i, 0)),
            full((SEQ_LAT, KV_DIM)),
            full((SEQ_LAT, KV_DIM)),
        ],
        out_specs=pl.BlockSpec((None, QB, Q_DIM), lambda b, i: (b, i, 0)),
        out_shape=jax.ShapeDtypeStruct((N_LAT_SEQ, SEQ_LAT, Q_DIM), F32),
        compiler_params=_cparams(("parallel", "parallel")),
        name="lat_attn",
    )(sink, q.reshape(T // SEQ_LAT, SEQ_LAT, Q_DIM), k.reshape(T // SEQ_LAT, SEQ_LAT, KV_DIM),
      v.reshape(T // SEQ_LAT, SEQ_LAT, KV_DIM), ck, cv, cq, sq, ckk, skk)
    return out.reshape(T_LAT, Q_DIM)


ST = 256
N_ST = T // ST


def _seq_tile_info(i):
    n_ctx = T_CTX // ST
    per_seq = SEQ_LAT // ST
    is_ctx = i < n_ctx
    pos0 = jnp.where(is_ctx, 0, ((i - n_ctx) % per_seq) * ST)
    slen = jnp.where(is_ctx, SEQ_CTX, SEQ_LAT)
    return pos0, slen


def _halo_specs(cols):
    return [
        pl.BlockSpec((ST, cols), lambda i: (jnp.maximum(i - 1, 0), 0)),
        pl.BlockSpec((ST, cols), lambda i: (i, 0)),
        pl.BlockSpec((ST, cols), lambda i: (jnp.minimum(i + 1, N_ST - 1), 0)),
    ]


def _with_halo(prev_ref, cur_ref, next_ref, halo, pos0, slen):
    has_prev = (pos0 > 0).astype(F32)
    has_next = (pos0 + ST < slen).astype(F32)
    return jnp.concatenate([prev_ref[ST - halo:, :] * has_prev, cur_ref[...], next_ref[:halo, :] * has_next], axis=0)


def _poolout_kernel(actx_ref, alat_ref, up_ref, uc_ref, un_ref, xc_ref, xl_ref, mod_ref, wp_ref, ps_ref, wo_ref,
                    o_ref):
    i = pl.program_id(0)
    pos0, slen = _seq_tile_info(i)
    halo = 8
    n = ST + 2 * halo
    ext = _with_halo(up_ref, uc_ref, un_ref, halo, pos0, slen)
    t = pos0 + lax.broadcasted_iota(jnp.int32, (ST, 1), 0)
    att = _two_part_tile(actx_ref, alat_ref, ST)
    out = jnp.dot(att.astype(BF16), wo_ref[:Q_DIM, :], preferred_element_type=F32)
    for g, w in enumerate(POOL_SIZES):
        sl = slice(g * PGD, (g + 1) * PGD)
        e = ext[:, sl]
        s = e + pltpu.roll(e, 1, 0)
        step = 1
        while 2 * step < w:
            s = pltpu.roll(s, step, 0) + pltpu.roll(s, n - step, 0)
            step *= 2
        lo = jnp.maximum(t - w // 2, 0)
        hi = jnp.minimum(t + (w - w // 2), slen)
        mean = s[halo:halo + ST, :] / (hi - lo).astype(F32)
        pooled = mean - e[halo:halo + ST, :]
        mixed = jnp.dot(pooled.astype(BF16), wp_ref[g], preferred_element_type=F32) * ps_ref[:, sl]
        out = out + jnp.dot(mixed.astype(BF16), wo_ref[Q_DIM + g * PGD:Q_DIM + (g + 1) * PGD, :],
                            preferred_element_type=F32)
    o_ref[...] = _two_part_tile(xc_ref, xl_ref, ST) + mod_ref[2:3, :] * out


def _poolout(att_ctx, att_lat, u, x_ctx, x_lat, mods, w_pool_b, pool_scale, w_out_b):
    return pl.pallas_call(
        _poolout_kernel,
        grid=(N_ST,),
        in_specs=_two_part_specs(ST, Q_DIM) + _halo_specs(POOL_DIM) + _two_part_specs(ST, D) + [
            pl.BlockSpec((None, 6, D), lambda i: (_group_of_tile(i, ST), 0, 0)),
            pl.BlockSpec((len(POOL_SIZES), PGD, PGD), lambda i: (0, 0, 0)),
            pl.BlockSpec((1, POOL_DIM), lambda i: (0, 0)),
            pl.BlockSpec((Q_DIM + POOL_DIM, D), lambda i: (0, 0)),
        ],
        out_specs=pl.BlockSpec((ST, D), lambda i: (i, 0)),
        out_shape=jax.ShapeDtypeStruct((T, D), F32),
        compiler_params=_cparams(("parallel",)),
        name="poolout",
    )(att_ctx, att_lat, u, u, u, x_ctx, x_lat, mods, w_pool_b, pool_scale, w_out_b)


def _glu_kernel(x_ref, g_ref, mod_ref, w_ref, b_ref, o_ref):
    h = _modnorm(x_ref[...], g_ref[...], mod_ref[0:1, :], mod_ref[1:2, :])
    a = jnp.dot(h.astype(BF16), w_ref[...], preferred_element_type=F32) + b_ref[...]
    o_ref[...] = a[:, :D] * jax.nn.sigmoid(a[:, D:])


def _glu(x, g, mods, w1_b, b1):
    tt = 512
    return pl.pallas_call(
        _glu_kernel,
        grid=(T // tt,),
        in_specs=[
            pl.BlockSpec((tt, D), lambda i: (i, 0)),
            pl.BlockSpec((1, D), lambda i: (0, 0)),
            pl.BlockSpec((None, 6, D), lambda i: (_group_of_tile(i, tt), 0, 0)),
            pl.BlockSpec((D, 2 * D), lambda i: (0, 0)),
            pl.BlockSpec((1, 2 * D), lambda i: (0, 0)),
        ],
        out_specs=pl.BlockSpec((tt, D), lambda i: (i, 0)),
        out_shape=jax.ShapeDtypeStruct((T, D), F32),
        compiler_params=_cparams(("parallel",)),
        name="glu",
    )(x, g, mods, w1_b, b1)


CONV_HALO = 16
SUBLANES = 8


def _convout_kernel(ap_ref, ac_ref, an_ref, x_ref, mod_ref, dw_ref, dwb_ref, lg_ref, lb_ref, w2_ref, b2_ref,
                    o_ref, ext_ref, sh_ref):
    i = pl.program_id(0)
    pos0, slen = _seq_tile_info(i)
    ext_ref[...] = _with_halo(ap_ref, ac_ref, an_ref, CONV_HALO, pos0, slen)
    n_sh = ST + 2 * CONV_HALO - SUBLANES
    for r in range(1, SUBLANES):
        sh_ref[r - 1] = ext_ref[pl.ds(r, n_sh), :]
    pad = CONV_W // 2
    half = ST // 2
    parts = []
    for c0 in range(0, D, 128):
        cs = slice(c0, c0 + 128)
        halves = []
        for r0 in (0, half):
            acc = jnp.zeros((half, 128), F32)
            for k in range(CONV_W):
                off = CONV_HALO - pad + k
                r = off % SUBLANES
                src = ext_ref if r == 0 else sh_ref.at[r - 1]
                acc = acc + src[pl.ds(off - r + r0, half), cs] * dw_ref[k:k + 1, cs]
            halves.append(acc)
        parts.append(jnp.concatenate(halves, axis=0))
    a = jnp.concatenate(parts, axis=1) + dwb_ref[...]
    mu = jnp.mean(a, axis=-1, keepdims=True)
    var = jnp.mean(jnp.square(a - mu), axis=-1, keepdims=True)
    y = (a - mu) * lax.rsqrt(var + EPS) * lg_ref[...] + lb_ref[...]
    y = jax.nn.silu(y)
    out = jnp.dot(y.astype(BF16), w2_ref[...], preferred_element_type=F32) + b2_ref[...]
    o_ref[...] = x_ref[...] + mod_ref[2:3, :] * out


def _convout(a, x, mods, dw, dwb, lg, lb, w2_b, b2):
    row = lambda: pl.BlockSpec((1, D), lambda i: (0, 0))
    return pl.pallas_call(
        _convout_kernel,
        grid=(N_ST,),
        in_specs=_halo_specs(D) + [
            pl.BlockSpec((ST, D), lambda i: (i, 0)),
            pl.BlockSpec((None, 6, D), lambda i: (_group_of_tile(i, ST), 0, 0)),
            pl.BlockSpec((CONV_W, D), lambda i: (0, 0)),
            row(), row(), row(),
            pl.BlockSpec((D, D), lambda i: (0, 0)),
            row(),
        ],
        out_specs=pl.BlockSpec((ST, D), lambda i: (i, 0)),
        out_shape=jax.ShapeDtypeStruct((T, D), F32),
        scratch_shapes=[pltpu.VMEM((ST + 2 * CONV_HALO, D), F32),
                        pltpu.VMEM((SUBLANES - 1, ST + 2 * CONV_HALO - SUBLANES, D), F32)],
        compiler_params=_cparams(("parallel",)),
        name="convout",
    )(a, a, a, x, mods, dw, dwb, lg, lb, w2_b, b2)


SEL_TT = 256
LANES = 128


def _topk16_exact(problems):
    vals = [[] for _ in problems]
    for r in range(P_TOPK):
        for i, (work_ref, rank_ref) in enumerate(problems):
            shape = work_ref.shape
            iota = lax.broadcasted_iota(jnp.int32, shape, 0).astype(F32)
            s = work_ref[...]
            m = jnp.max(s, axis=0, keepdims=True)
            hit = iota == jnp.min(jnp.where(s == m, iota, float(shape[0])), axis=0, keepdims=True)
            pltpu.store(work_ref, jnp.full(shape, -INF, F32), mask=hit)
            pltpu.store(rank_ref, jnp.full(shape, float(r), F32), mask=hit)
            vals[i].append(m)
    return [jnp.concatenate(v, axis=0) for v in vals]


def _compare_exchange(t, hi, lo):
    if t[lo] is None:
        return
    if t[hi] is None:
        t[hi], t[lo] = t[lo], None
        return
    t[hi], t[lo] = jnp.maximum(t[hi], t[lo]), jnp.minimum(t[hi], t[lo])


def _top16_sorted(scores):
    n = P_TOPK
    n_tiles = scores.shape[0] // SUBLANES
    t = [scores[SUBLANES * i:SUBLANES * (i + 1), :] if i < n_tiles else None for i in range(n)]
    k = 2
    while k <= n:
        j = k // 2
        while j >= 1:
            for i in range(n):
                m = i ^ j
                if m > i:
                    if i & k == 0:
                        _compare_exchange(t, i, m)
                    else:
                        _compare_exchange(t, m, i)
            j //= 2
        k *= 2
    for shift in (4, 2, 1):
        other = [None if x is None else pltpu.roll(x, shift, 0) for x in t]

        def larger(a, b):
            return b if a is None else a if b is None else jnp.maximum(a, b)

        t = [larger(t[i], other[n - 1 - i]) for i in range(n)]
        j = n // 2
        while j >= 1:
            for i in range(n):
                if i ^ j > i:
                    _compare_exchange(t, i, i ^ j)
            j //= 2
    return jnp.concatenate([x[0:1, :] for x in t], axis=0)


def _count_ge(s, thr):
    return jnp.sum((s >= thr).astype(F32), axis=0, keepdims=True)


_CAND_BLOCKS = [(0, 0, 8), (0, 8, 8)] + [(r1, 0, P_TOPK // (r1 + 1)) for r1 in range(1, 8)] + [(None, 0, 8)]


N_CAND = 8 * len(_CAND_BLOCKS)
HEADS_PER_TRIP = 4


def _select_kernel(x_ref, g_ref, mod_ref, wq_ref, keys_ref, hb_ref, r_ref, e2_ref, c_ref, e1_ref,
                   q_scr, s_scr, wk_scr, rk_scr, cw_scr, cr_scr, ce_scr):
    h = _modnorm(x_ref[...], g_ref[...], mod_ref[3:4, :], mod_ref[4:5, :])
    hb = h.astype(BF16)
    hb_ref[...] = h.T.astype(BF16)
    q_scr[...] = jnp.dot(hb, wq_ref[...], preferred_element_type=F32).astype(BF16)
    iota8 = lax.broadcasted_iota(jnp.int32, (8, LANES), 0)

    n_lc = SEL_TT // LANES

    def head_group(i, carry):
        for j in range(HEADS_PER_TRIP):
            hd = i * HEADS_PER_TRIP + j
            for p in range(2):
                col = pl.multiple_of((hd * 2 + p) * N_KEYS, N_KEYS)
                s_scr[j, p] = lax.dot_general(keys_ref[hd * 2 + p], q_scr[:, pl.ds(col, N_KEYS)], _NT,
                                              preferred_element_type=F32)
        tied = select_group(i, exact=False)

        @pl.when(jnp.max(tied) > 0.0)
        def _():
            select_group(i, exact=True)

        return carry

    def select_group(i, exact):
        probs = [(i * HEADS_PER_TRIP + j, j, lc, j * n_lc + lc) for j in range(HEADS_PER_TRIP) for lc in range(n_lc)]
        tied = jnp.zeros((1, LANES), F32)
        for hd, j, lc, sc in probs:
            ls = slice(lc * LANES, (lc + 1) * LANES)
            for p in range(2):
                wk_scr[sc, p] = s_scr[j, p, :, ls]
                rk_scr[sc, p] = jnp.full((N_KEYS, LANES), float(P_TOPK), F32)
        if exact:
            tops = _topk16_exact([(wk_scr.at[sc, p], rk_scr.at[sc, p]) for _, _, _, sc in probs for p in range(2)])
        else:
            tops = [_top16_sorted(wk_scr[sc, p]) for _, _, _, sc in probs for p in range(2)]
        for n, (hd, j, lc, sc) in enumerate(probs):
            v1, v2 = tops[2 * n], tops[2 * n + 1]
            for b, (r1, r2, nvalid) in enumerate(_CAND_BLOCKS):
                if r1 is None:
                    blk = v1[8:16, :] + v2[0:1, :]
                else:
                    blk = v1[r1:r1 + 1, :] + v2[r2:r2 + 8, :]
                if nvalid < 8:
                    blk = jnp.where(iota8 < nvalid, blk, -INF)
                cw_scr[sc, 8 * b:8 * b + 8, :] = blk
                ce_scr[sc, 8 * b:8 * b + 8, :] = jnp.exp(blk - (v1[0:1, :] + v2[0:1, :]))
            if exact:
                cr_scr[sc] = jnp.full((N_CAND, LANES), float(P_TOPK), F32)
        if exact:
            _topk16_exact([(cw_scr.at[sc], cr_scr.at[sc]) for _, _, _, sc in probs])
        else:
            ctops = [_top16_sorted(cw_scr[sc]) for _, _, _, sc in probs]
        for n, (hd, j, lc, sc) in enumerate(probs):
            ls = slice(lc * LANES, (lc + 1) * LANES)
            v1, v2 = tops[2 * n], tops[2 * n + 1]
            s1 = s_scr[j, 0, :, ls]
            s2 = s_scr[j, 1, :, ls]
            if exact:
                rank1 = rk_scr[sc, 0]
                sel = cr_scr[sc] < float(P_TOPK)
            else:
                sel = cw_scr[sc] >= ctops[n][P_TOPK - 1:P_TOPK, :]
                for cnt_ge in (_count_ge(s1, v1[P_TOPK - 1:P_TOPK, :]), _count_ge(s2, v2[P_TOPK - 1:P_TOPK, :]),
                               jnp.sum(sel.astype(F32), axis=0, keepdims=True)):
                    tied = jnp.maximum(tied, (cnt_ge != float(P_TOPK)).astype(F32))
                for v in (v1, v2):
                    repeats = (v[:P_TOPK - 1, :] == v[1:, :]).astype(F32)
                    tied = jnp.maximum(tied, jnp.max(repeats, axis=0, keepdims=True))
            self32 = sel.astype(F32)
            z = jnp.sum(jnp.where(sel, ce_scr[sc], 0.0), axis=0, keepdims=True)
            cnt_rows = [jnp.sum(self32[0:16, :], axis=0, keepdims=True)]
            for b in range(2, 9):
                cnt_rows.append(jnp.sum(self32[8 * b:8 * b + 8, :], axis=0, keepdims=True))
            cnt = jnp.concatenate(cnt_rows + [self32[72:80, :]], axis=0)
            if exact:
                c_ref[hd, lc] = jnp.zeros((N_KEYS, LANES), F32)
                for r in range(P_TOPK):
                    pltpu.store(c_ref.at[hd, lc], jnp.broadcast_to(cnt[r:r + 1, :], (N_KEYS, LANES)),
                                mask=rank1 == float(r))
                rank2 = rk_scr[sc, 1]
            else:
                kept = jnp.zeros((N_KEYS, LANES), F32)
                rank2 = jnp.full((N_KEYS, LANES), float(P_TOPK), F32)
                for r in range(P_TOPK):
                    kept = jnp.where(s1 == v1[r:r + 1, :], cnt[r:r + 1, :], kept)
                    rank2 = jnp.where(s2 == v2[r:r + 1, :], float(r), rank2)
                c_ref[hd, lc] = kept
            r_ref[hd, :, ls] = rank2.astype(BF16)
            e2_ref[hd, :, ls] = jnp.exp(s2 - v2[0:1, :]).astype(BF16)
            e1_ref[hd, lc] = jnp.exp(s1 - v1[0:1, :]) * (0.5 / z)
        return tied

    lax.fori_loop(0, P_HEADS // HEADS_PER_TRIP, head_group, 0)


def _select(x, g, mods, wq_b, keys_b):
    tt = SEL_TT
    hk = pl.BlockSpec((P_HEADS, N_KEYS, tt), lambda i: (0, 0, i))
    hs = pl.BlockSpec((P_HEADS, tt // LANES, N_KEYS, LANES), lambda i: (0, i, 0, 0))
    n_slot = HEADS_PER_TRIP * (tt // LANES)
    return pl.pallas_call(
        _select_kernel,
        grid=(T // tt,),
        in_specs=[
            pl.BlockSpec((tt, D), lambda i: (i, 0)),
            pl.BlockSpec((1, D), lambda i: (0, 0)),
            pl.BlockSpec((None, 6, D), lambda i: (_group_of_tile(i, tt), 0, 0)),
            pl.BlockSpec((D, 2 * P_HEADS * N_KEYS), lambda i: (0, 0)),
            pl.BlockSpec((2 * P_HEADS, N_KEYS, N_KEYS), lambda i: (0, 0, 0)),
        ],
        out_specs=[pl.BlockSpec((D, tt), lambda i: (0, i)), hk, hk, hs, hs],
        out_shape=[
            jax.ShapeDtypeStruct((D, T), BF16),
            jax.ShapeDtypeStruct((P_HEADS, N_KEYS, T), BF16),
            jax.ShapeDtypeStruct((P_HEADS, N_KEYS, T), BF16),
            jax.ShapeDtypeStruct((P_HEADS, T // LANES, N_KEYS, LANES), F32),
            jax.ShapeDtypeStruct((P_HEADS, T // LANES, N_KEYS, LANES), F32),
        ],
        scratch_shapes=[
            pltpu.VMEM((tt, 2 * P_HEADS * N_KEYS), BF16),
            pltpu.VMEM((HEADS_PER_TRIP, 2, N_KEYS, tt), F32),
            pltpu.VMEM((n_slot, 2, N_KEYS, LANES), F32),
            pltpu.VMEM((n_slot, 2, N_KEYS, LANES), F32),
            pltpu.VMEM((n_slot, N_CAND, LANES), F32),
            pltpu.VMEM((n_slot, N_CAND, LANES), F32),
            pltpu.VMEM((n_slot, N_CAND, LANES), F32),
        ],
        compiler_params=_cparams(("parallel",)),
        name="peer_select",
    )(x, g, mods, wq_b, keys_b)


DT = 512
DE = 2048
I1_PER = DE // N_KEYS
PACK = 16
DSUB = 256


def _row_replicated(ref, hd, sub, ii):
    slabs = [jnp.broadcast_to(ref[hd, sub * (DSUB // LANES) + s, ii:ii + 1, :], (PACK, LANES))
             for s in range(DSUB // LANES)]
    return jnp.concatenate(slabs, axis=1)


def _dense_kernel(final_norm, hb_ref, u_ref, vt_ref, r_ref, e2_ref, c_ref, e1_ref, x_ref, mod_ref, fg_ref, *refs):
    *out_refs, acc_ref, p_ref = refs
    k = pl.program_id(1)

    @pl.when(k == 0)
    def _():
        acc_ref[...] = jnp.zeros_like(acc_ref)

    def scores(sub):
        return jnp.dot(u_ref[...], hb_ref[:, sub * DSUB:(sub + 1) * DSUB], preferred_element_type=F32)

    n_sub = DT // DSUB
    a_next = scores(0)
    for sub in range(n_sub):
        ts = slice(sub * DSUB, (sub + 1) * DSUB)
        a_t = a_next
        if sub + 1 < n_sub:
            a_next = scores(sub + 1)
        act = (a_t * (1.0 + lax.erf(a_t * math.sqrt(0.5)))).astype(BF16)
        for ii in range(I1_PER):
            w = jnp.zeros((N_KEYS // PACK, PACK, DSUB), BF16)
            for hd in range(P_HEADS):
                cb = _row_replicated(c_ref, hd, sub, ii).astype(BF16)
                eb = _row_replicated(e1_ref, hd, sub, ii).astype(BF16)
                w = w + jnp.where(r_ref[hd, :, :, ts] < cb[None], e2_ref[hd, :, :, ts], jnp.zeros((), BF16)) * eb[None]
            rows = slice(ii * N_KEYS, (ii + 1) * N_KEYS)
            p_ref[rows, ts] = act[rows, :] * w.reshape(N_KEYS, DSUB)
        acc_ref[:, ts] += jnp.dot(vt_ref[...], p_ref[:, ts], preferred_element_type=F32)

    @pl.when(k == pl.num_programs(1) - 1)
    def _():
        y = x_ref[...] + mod_ref[5:6, :] * acc_ref[...].T
        if not final_norm:
            out_refs[0][...] = y
        else:
            y = y * lax.rsqrt(jnp.mean(y * y, axis=-1, keepdims=True) + EPS) * fg_ref[...]
            is_ctx = pl.program_id(0) < T_CTX // DT

            @pl.when(is_ctx)
            def _():
                out_refs[0][...] = y

            @pl.when(jnp.logical_not(is_ctx))
            def _():
                out_refs[1][...] = y


def _dense(layer, hb, u_tab, vt_b, rk, e2, c, e1, x, mods, fg, final_norm):
    hk_b = pl.BlockSpec((P_HEADS, N_KEYS // PACK, PACK, DT), lambda j, k: (0, 0, 0, j))
    hk_f = pl.BlockSpec((P_HEADS, DT // LANES, I1_PER, LANES), lambda j, k: (0, j, k, 0))
    rk4 = rk.reshape(P_HEADS, N_KEYS // PACK, PACK, T)
    e24 = e2.reshape(P_HEADS, N_KEYS // PACK, PACK, T)
    if final_norm:
        n_ctx = T_CTX // DT
        out_specs = [pl.BlockSpec((DT, D), lambda j, k: (jnp.minimum(j, n_ctx - 1), 0)),
                     pl.BlockSpec((DT, D), lambda j, k: (jnp.maximum(j - n_ctx, 0), 0))]
        out_shape = [jax.ShapeDtypeStruct((T_CTX, D), F32), jax.ShapeDtypeStruct((T_LAT, D), F32)]
    else:
        out_specs = pl.BlockSpec((DT, D), lambda j, k: (j, 0))
        out_shape = jax.ShapeDtypeStruct((T, D), F32)
    return pl.pallas_call(
        functools.partial(_dense_kernel, final_norm),
        grid=(T // DT, N_EXPERTS // DE),
        in_specs=[
            pl.BlockSpec((D, DT), lambda j, k: (0, j)),
            pl.BlockSpec((None, DE, D), lambda j, k: (layer, k, 0)),
            pl.BlockSpec((None, D, DE), lambda j, k: (layer, 0, k)),
            hk_b, hk_b, hk_f, hk_f,
            pl.BlockSpec((DT, D), lambda j, k: (j, 0)),
            pl.BlockSpec((None, 6, D), lambda j, k: (_group_of_tile(j, DT), 0, 0)),
            pl.BlockSpec((1, D), lambda j, k: (0, 0)),
        ],
        out_specs=out_specs,
        out_shape=out_shape,
        scratch_shapes=[pltpu.VMEM((D, DT), F32), pltpu.VMEM((DE, DT), BF16)],
        compiler_params=_cparams(("arbitrary", "arbitrary")),
        name="peer_dense",
    )(hb, u_tab, vt_b, rk4, e24, c, e1, x, mods, fg)


def _transpose_cast_kernel(v_ref, o_ref):
    o_ref[...] = v_ref[...].T.astype(BF16)


def _value_tables(peer_v):
    n_layers = peer_v.shape[0]
    blk = 1024
    return pl.pallas_call(
        _transpose_cast_kernel,
        grid=(n_layers, N_EXPERTS // blk),
        in_specs=[pl.BlockSpec((None, blk, D), lambda l, e: (l, e, 0))],
        out_specs=pl.BlockSpec((None, D, blk), lambda l, e: (l, 0, e)),
        out_shape=jax.ShapeDtypeStruct((n_layers, D, N_EXPERTS), BF16),
        compiler_params=_cparams(("parallel", "parallel")),
        name="value_tables",
    )(peer_v)


def _peer(layer, x, g, mods, wq, keys, u_tab, vt_b, fg, final_norm):
    hb, rk, e2, c, e1 = _select(x, g, mods, wq.astype(BF16),
                                keys.reshape(2 * P_HEADS, N_KEYS, N_KEYS).astype(BF16))
    return _dense(layer, hb, u_tab, vt_b, rk, e2, c, e1, x, mods, fg, final_norm)


def kernel(x_prompt, x_sample, c, cache_k, cache_v, c_ctx, mod_w, mod_b, norm_mix_g, norm_ffn_g, w_in, attn_sink, w_pool, pool_scale, w_out, conv_w1, conv_b1, conv_dw, conv_dw_b, conv_ln_g, conv_ln_b, conv_w2, conv_b2, peer_wq, peer_keys, peer_u, peer_v, final_norm_g):
    x_ctx, x_lat = x_prompt.reshape(T_CTX, D), x_sample.reshape(T_LAT, D)
    cvec = jnp.concatenate([c_ctx[None, :], c, jnp.zeros((N_GROUPS - 1 - N_LAT_SEQ, D), F32)], axis=0)
    mods_all = _adaln(cvec, mod_w, mod_b).reshape(mod_w.shape[0], N_GROUPS, 6, D)
    row = lambda a: a.reshape(1, -1)
    fg = row(final_norm_g)

    mods = mods_all[0]
    q, k, v, u = _inproj(x_ctx, x_lat, row(norm_mix_g[0]), mods, w_in[0].astype(BF16))
    att_ctx = _ctx_attn(attn_sink[0], q, k, v)
    n_past = cache_k.shape[2]
    att_lat = _lat_attn(attn_sink[0], q, k, v,
                        cache_k[:, 0].reshape(N_LAT_SEQ, n_past, KV_DIM),
                        cache_v[:, 0].reshape(N_LAT_SEQ, n_past, KV_DIM))
    x = _poolout(att_ctx, att_lat, u, x_ctx, x_lat, mods, w_pool[0].astype(BF16), row(pool_scale[0]), w_out[0].astype(BF16))
    vt_b = _value_tables(peer_v)
    x = _peer(0, x, row(norm_ffn_g[0]), mods, peer_wq[0], peer_keys[0], peer_u, vt_b, fg, False)
    state_k = k[:T_CTX].reshape(N_CTX_SEQ, 1, SEQ_CTX, N_KV, HD)
    state_v = v[:T_CTX].reshape(N_CTX_SEQ, 1, SEQ_CTX, N_KV, HD)

    mods = mods_all[1]
    a = _glu(x, row(norm_mix_g[1]), mods, conv_w1[0].astype(BF16), row(conv_b1[0]))
    x = _convout(a, x, mods, conv_dw[0], row(conv_dw_b[0]), row(conv_ln_g[0]), row(conv_ln_b[0]),
                 conv_w2[0].astype(BF16), row(conv_b2[0]))
    y_ctx, y_lat = _peer(1, x, row(norm_ffn_g[1]), mods, peer_wq[1], peer_keys[1], peer_u, vt_b, fg, True)

    y_prompt = y_ctx.reshape(N_CTX_SEQ, SEQ_CTX, D)
    y_sample = y_lat.reshape(N_LAT_SEQ, SEQ_LAT, D)
    return (y_prompt, y_sample, state_k, state_v)
```

```python
import functools
import math

import jax
import jax.numpy as jnp
from jax import lax
from jax.experimental import pallas as pl
from jax.experimental.pallas import tpu as pltpu

F32 = jnp.float32
BF16 = jnp.bfloat16

D = 1024
N_CTX_SEQ = 16
SEQ_CTX = 256
N_LAT_SEQ = 2
SEQ_LAT = 2048
T_CTX = N_CTX_SEQ * SEQ_CTX
T_LAT = N_LAT_SEQ * SEQ_LAT
T = T_CTX + T_LAT
N_GROUPS = 8
GRID_W = 64
ROPE_BASE = 10000.0
N_HEADS = 8
N_KV = 2
GROUPS = N_HEADS // N_KV
HD = 64
WINDOW = 128
Q_DIM = N_HEADS * HD
KV_DIM = N_KV * HD
POOL_SIZES = (2, 4, 8, 16)
POOL_DIM = 512
PGD = 128
IN_DIM = Q_DIM + 2 * KV_DIM + POOL_DIM
CONV_W = 31
N_KEYS = 128
N_EXPERTS = N_KEYS * N_KEYS
P_HEADS = 8
P_TOPK = 16
EPS = 1e-6
NEG = -1e30
INF = float("inf")

VMEM_LIMIT = 56 * 1024 * 1024


def _cparams(sem, flags=None):
    return pltpu.CompilerParams(dimension_semantics=sem, vmem_limit_bytes=VMEM_LIMIT, flags=flags)


def _group_of_tile(i, tile):
    n_ctx = T_CTX // tile
    per_seq = SEQ_LAT // tile
    return jnp.where(i < n_ctx, 0, 1 + (i - n_ctx) // per_seq)


def _modnorm(x, g, shift, scale):
    y = x * lax.rsqrt(jnp.mean(x * x, axis=-1, keepdims=True) + EPS)
    y = y * g
    return y * (1 + scale) + shift


def _adaln_kernel(cv_ref, w_ref, b_ref, o_ref):
    a = jax.nn.silu(cv_ref[...]).astype(BF16)
    o_ref[...] = jnp.dot(a, w_ref[...].astype(BF16), preferred_element_type=F32) + b_ref[...]


def _adaln(cvec, mod_w, mod_b):
    L, _, n6 = mod_w.shape
    tn = 1536
    return pl.pallas_call(
        _adaln_kernel,
        grid=(L, n6 // tn),
        in_specs=[
            pl.BlockSpec((N_GROUPS, D), lambda l, n: (0, 0)),
            pl.BlockSpec((None, D, tn), lambda l, n: (l, 0, n)),
            pl.BlockSpec((None, 1, tn), lambda l, n: (l, 0, n)),
        ],
        out_specs=pl.BlockSpec((None, N_GROUPS, tn), lambda l, n: (l, 0, n)),
        out_shape=jax.ShapeDtypeStruct((L, N_GROUPS, n6), F32),
        compiler_params=_cparams(("parallel", "parallel")),
        name="adaln",
    )(cvec, mod_w, mod_b.reshape(L, 1, n6))


def _two_part_specs(tile, cols):
    n_ctx = T_CTX // tile
    return [pl.BlockSpec((tile, cols), lambda i: (jnp.minimum(i, n_ctx - 1), 0)),
            pl.BlockSpec((tile, cols), lambda i: (jnp.maximum(i - n_ctx, 0), 0))]


def _two_part_tile(ctx_ref, lat_ref, tile):
    return jnp.where(pl.program_id(0) < T_CTX // tile, ctx_ref[...], lat_ref[...])


def _inproj_kernel(xc_ref, xl_ref, g_ref, mod_ref, w_ref, q_ref, k_ref, v_ref, u_ref):
    h = _modnorm(_two_part_tile(xc_ref, xl_ref, INPROJ_TT), g_ref[...], mod_ref[0:1, :], mod_ref[1:2, :])
    p = jnp.dot(h.astype(BF16), w_ref[...], preferred_element_type=F32)
    q_ref[...] = p[:, :Q_DIM]
    k_ref[...] = p[:, Q_DIM:Q_DIM + KV_DIM]
    v_ref[...] = p[:, Q_DIM + KV_DIM:Q_DIM + 2 * KV_DIM]
    u_ref[...] = p[:, Q_DIM + 2 * KV_DIM:]


INPROJ_TT = 512


def _inproj(x_ctx, x_lat, g, mods, w_in_b):
    tt = INPROJ_TT
    return pl.pallas_call(
        _inproj_kernel,
        grid=(T // tt,),
        in_specs=_two_part_specs(tt, D) + [
            pl.BlockSpec((1, D), lambda i: (0, 0)),
            pl.BlockSpec((None, 6, D), lambda i: (_group_of_tile(i, tt), 0, 0)),
            pl.BlockSpec((D, IN_DIM), lambda i: (0, 0)),
        ],
        out_specs=[
            pl.BlockSpec((tt, Q_DIM), lambda i: (i, 0)),
            pl.BlockSpec((tt, KV_DIM), lambda i: (i, 0)),
            pl.BlockSpec((tt, KV_DIM), lambda i: (i, 0)),
            pl.BlockSpec((tt, POOL_DIM), lambda i: (i, 0)),
        ],
        out_shape=[
            jax.ShapeDtypeStruct((T, Q_DIM), F32),
            jax.ShapeDtypeStruct((T, KV_DIM), F32),
            jax.ShapeDtypeStruct((T, KV_DIM), F32),
            jax.ShapeDtypeStruct((T, POOL_DIM), F32),
        ],
        compiler_params=_cparams(("parallel",)),
        name="inproj",
    )(x_ctx, x_lat, g, mods, w_in_b)


def _softmax_parts(parts, sk):
    m = sk
    for s in parts:
        m = jnp.maximum(m, jnp.max(s, axis=-1, keepdims=True))
    es = [jnp.exp(s - m) for s in parts]
    den = jnp.exp(sk - m)
    for e in es:
        den = den + jnp.sum(e, axis=-1, keepdims=True)
    return [(e / den).astype(BF16) for e in es]


_NT = (((1,), (1,)), ((), ()))


def _ctx_attn_kernel(sink_ref, q_ref, k_ref, v_ref, o_ref):
    scale = HD ** -0.5
    for j in range(N_KV):
        kj = k_ref[:, j * HD:(j + 1) * HD].astype(BF16)
        vj = v_ref[:, j * HD:(j + 1) * HD].astype(BF16)
        for g in range(GROUPS):
            hd = j * GROUPS + g
            qh = q_ref[:, hd * HD:(hd + 1) * HD].astype(BF16)
            s = lax.dot_general(qh, kj, _NT, preferred_element_type=F32) * scale
            (p,) = _softmax_parts([s], sink_ref[hd])
            o_ref[:, hd * HD:(hd + 1) * HD] = jnp.dot(p, vj, preferred_element_type=F32)


def _ctx_attn(sink, q, k, v):
    return pl.pallas_call(
        _ctx_attn_kernel,
        grid=(N_CTX_SEQ,),
        in_specs=[
            pl.BlockSpec(memory_space=pltpu.SMEM),
            pl.BlockSpec((SEQ_CTX, Q_DIM), lambda b: (b, 0)),
            pl.BlockSpec((SEQ_CTX, KV_DIM), lambda b: (b, 0)),
            pl.BlockSpec((SEQ_CTX, KV_DIM), lambda b: (b, 0)),
        ],
        out_specs=pl.BlockSpec((SEQ_CTX, Q_DIM), lambda b: (b, 0)),
        out_shape=jax.ShapeDtypeStruct((T_CTX, Q_DIM), F32),
        compiler_params=_cparams(("parallel",)),
        name="ctx_attn",
    )(sink, q, k, v)


def _rope(x, cos, sin_signed):
    n = x.shape[-1]
    lane = lax.broadcasted_iota(jnp.int32, x.shape, 1)
    up = pltpu.roll(x, n - 16, 1)
    dn = pltpu.roll(x, 16, 1)
    partner = jnp.where((lane & 16) == 0, up, dn)
    return x * cos + partner * sin_signed


QB = 128
KWIN = QB + 2 * WINDOW


def _lat_attn_kernel(sink_ref, q_ref, k_ref, v_ref, ck_ref, cv_ref, cq_ref, sq_ref, ckk_ref, skk_ref, o_ref):
    scale = HD ** -0.5
    qb = pl.program_id(1)
    start = pl.multiple_of(jnp.clip(qb * QB - WINDOW, 0, SEQ_LAT - KWIN), WINDOW)
    q = _rope(q_ref[...], cq_ref[...], sq_ref[...])
    kw = _rope(k_ref[pl.ds(start, KWIN), :], ckk_ref[pl.ds(start, KWIN), :], skk_ref[pl.ds(start, KWIN), :])
    vw = v_ref[pl.ds(start, KWIN), :]
    rows = GROUPS * QB
    qpos = qb * QB + lax.broadcasted_iota(jnp.int32, (rows, KWIN), 0) % QB
    kpos = start + lax.broadcasted_iota(jnp.int32, (rows, KWIN), 1)
    valid = jnp.abs(qpos - kpos) <= WINDOW
    for j in range(N_KV):
        sl = slice(j * HD, (j + 1) * HD)
        kj = kw[:, sl].astype(BF16)
        vj = vw[:, sl].astype(BF16)
        ckj = ck_ref[:, sl].astype(BF16)
        cvj = cv_ref[:, sl].astype(BF16)
        heads = [j * GROUPS + g for g in range(GROUPS)]
        qs = jnp.concatenate([q[:, hd * HD:(hd + 1) * HD] for hd in heads], axis=0).astype(BF16)
        sk = jnp.concatenate([jnp.full((QB, 1), sink_ref[hd], F32) for hd in heads], axis=0)
        s_ctx = lax.dot_general(qs, ckj, _NT, preferred_element_type=F32) * scale
        s_loc = lax.dot_general(qs, kj, _NT, preferred_element_type=F32) * scale
        s_loc = jnp.where(valid, s_loc, NEG)
        p_ctx, p_loc = _softmax_parts([s_ctx, s_loc], sk)
        o = jnp.dot(p_ctx, cvj, preferred_element_type=F32) + jnp.dot(p_loc, vj, preferred_element_type=F32)
        for g, hd in enumerate(heads):
            o_ref[:, hd * HD:(hd + 1) * HD] = o[g * QB:(g + 1) * QB, :]


def _rope_tables():
    pos = jnp.arange(SEQ_LAT)
    rows = (pos // GRID_W).astype(F32)
    cols = (pos % GRID_W).astype(F32)
    quarter = HD // 4
    freqs = ROPE_BASE ** (-jnp.arange(quarter, dtype=F32) / quarter)
    ar = rows[:, None] * freqs[None, :]
    ac = cols[:, None] * freqs[None, :]
    cos = jnp.concatenate([jnp.cos(ar), jnp.cos(ar), jnp.cos(ac), jnp.cos(ac)], axis=-1)
    sin = jnp.concatenate([-jnp.sin(ar), jnp.sin(ar), -jnp.sin(ac), jnp.sin(ac)], axis=-1)
    return cos, sin


def _lat_attn(sink, q, k, v, ck, cv):
    cos, sin = _rope_tables()
    cq, sq = jnp.tile(cos, (1, N_HEADS)), jnp.tile(sin, (1, N_HEADS))
    ckk, skk = jnp.tile(cos, (1, N_KV)), jnp.tile(sin, (1, N_KV))
    b0 = T_CTX // SEQ_LAT
    full = lambda shape: pl.BlockSpec(shape, lambda b, i: (0, 0))
    per_b = lambda rows, cols: pl.BlockSpec((None, rows, cols), lambda b, i: (b, 0, 0))
    tok_b = lambda cols: pl.BlockSpec((None, SEQ_LAT, cols), lambda b, i: (b0 + b, 0, 0))
    out = pl.pallas_call(
        _lat_attn_kernel,
        grid=(N_LAT_SEQ, SEQ_LAT // QB),
        in_specs=[
            pl.BlockSpec(memory_space=pltpu.SMEM),
            pl.BlockSpec((None, QB, Q_DIM), lambda b, i: (b0 + b, i, 0)),
            tok_b(KV_DIM),
            tok_b(KV_DIM),
            per_b(ck.shape[1], KV_DIM),
            per_b(cv.shape[1], KV_DIM),
            pl.BlockSpec((QB, Q_DIM), lambda b, i: (i, 0)),
            pl.BlockSpec((QB, Q_DIM), lambda b, i: (i, 0)),
            full((SEQ_LAT, KV_DIM)),
            full((SEQ_LAT, KV_DIM)),
        ],
        out_specs=pl.BlockSpec((None, QB, Q_DIM), lambda b, i: (b, i, 0)),
        out_shape=jax.ShapeDtypeStruct((N_LAT_SEQ, SEQ_LAT, Q_DIM), F32),
        compiler_params=_cparams(("parallel", "parallel")),
        name="lat_attn",
    )(sink, q.reshape(T // SEQ_LAT, SEQ_LAT, Q_DIM), k.reshape(T // SEQ_LAT, SEQ_LAT, KV_DIM),
      v.reshape(T // SEQ_LAT, SEQ_LAT, KV_DIM), ck, cv, cq, sq, ckk, skk)
    return out.reshape(T_LAT, Q_DIM)


ST = 256
N_ST = T // ST


def _seq_tile_info(i):
    n_ctx = T_CTX // ST
    per_seq = SEQ_LAT // ST
    is_ctx = i < n_ctx
    pos0 = jnp.where(is_ctx, 0, ((i - n_ctx) % per_seq) * ST)
    slen = jnp.where(is_ctx, SEQ_CTX, SEQ_LAT)
    return pos0, slen


def _halo_specs(cols):
    return [
        pl.BlockSpec((ST, cols), lambda i: (jnp.maximum(i - 1, 0), 0)),
        pl.BlockSpec((ST, cols), lambda i: (i, 0)),
        pl.BlockSpec((ST, cols), lambda i: (jnp.minimum(i + 1, N_ST - 1), 0)),
    ]


def _with_halo(prev_ref, cur_ref, next_ref, halo, pos0, slen):
    has_prev = (pos0 > 0).astype(F32)
    has_next = (pos0 + ST < slen).astype(F32)
    return jnp.concatenate([prev_ref[ST - halo:, :] * has_prev, cur_ref[...], next_ref[:halo, :] * has_next], axis=0)


def _poolout_kernel(actx_ref, alat_ref, up_ref, uc_ref, un_ref, xc_ref, xl_ref, mod_ref, wp_ref, ps_ref, wo_ref,
                    o_ref):
    i = pl.program_id(0)
    pos0, slen = _seq_tile_info(i)
    halo = 8
    n = ST + 2 * halo
    ext = _with_halo(up_ref, uc_ref, un_ref, halo, pos0, slen)
    t = pos0 + lax.broadcasted_iota(jnp.int32, (ST, 1), 0)
    att = _two_part_tile(actx_ref, alat_ref, ST)
    out = jnp.dot(att.astype(BF16), wo_ref[:Q_DIM, :], preferred_element_type=F32)
    for g, w in enumerate(POOL_SIZES):
        sl = slice(g * PGD, (g + 1) * PGD)
        e = ext[:, sl]
        s = e + pltpu.roll(e, 1, 0)
        step = 1
        while 2 * step < w:
            s = pltpu.roll(s, step, 0) + pltpu.roll(s, n - step, 0)
            step *= 2
        lo = jnp.maximum(t - w // 2, 0)
        hi = jnp.minimum(t + (w - w // 2), slen)
        mean = s[halo:halo + ST, :] / (hi - lo).astype(F32)
        pooled = mean - e[halo:halo + ST, :]
        mixed = jnp.dot(pooled.astype(BF16), wp_ref[g], preferred_element_type=F32) * ps_ref[:, sl]
        out = out + jnp.dot(mixed.astype(BF16), wo_ref[Q_DIM + g * PGD:Q_DIM + (g + 1) * PGD, :],
                            preferred_element_type=F32)
    o_ref[...] = _two_part_tile(xc_ref, xl_ref, ST) + mod_ref[2:3, :] * out


def _poolout(att_ctx, att_lat, u, x_ctx, x_lat, mods, w_pool_b, pool_scale, w_out_b):
    return pl.pallas_call(
        _poolout_kernel,
        grid=(N_ST,),
        in_specs=_two_part_specs(ST, Q_DIM) + _halo_specs(POOL_DIM) + _two_part_specs(ST, D) + [
            pl.BlockSpec((None, 6, D), lambda i: (_group_of_tile(i, ST), 0, 0)),
            pl.BlockSpec((len(POOL_SIZES), PGD, PGD), lambda i: (0, 0, 0)),
            pl.BlockSpec((1, POOL_DIM), lambda i: (0, 0)),
            pl.BlockSpec((Q_DIM + POOL_DIM, D), lambda i: (0, 0)),
        ],
        out_specs=pl.BlockSpec((ST, D), lambda i: (i, 0)),
        out_shape=jax.ShapeDtypeStruct((T, D), F32),
        compiler_params=_cparams(("parallel",)),
        name="poolout",
    )(att_ctx, att_lat, u, u, u, x_ctx, x_lat, mods, w_pool_b, pool_scale, w_out_b)


def _glu_kernel(x_ref, g_ref, mod_ref, w_ref, b_ref, o_ref):
    h = _modnorm(x_ref[...], g_ref[...], mod_ref[0:1, :], mod_ref[1:2, :])
    a = jnp.dot(h.astype(BF16), w_ref[...], preferred_element_type=F32) + b_ref[...]
    o_ref[...] = a[:, :D] * jax.nn.sigmoid(a[:, D:])


def _glu(x, g, mods, w1_b, b1):
    tt = 512
    return pl.pallas_call(
        _glu_kernel,
        grid=(T // tt,),
        in_specs=[
            pl.BlockSpec((tt, D), lambda i: (i, 0)),
            pl.BlockSpec((1, D), lambda i: (0, 0)),
            pl.BlockSpec((None, 6, D), lambda i: (_group_of_tile(i, tt), 0, 0)),
            pl.BlockSpec((D, 2 * D), lambda i: (0, 0)),
            pl.BlockSpec((1, 2 * D), lambda i: (0, 0)),
        ],
        out_specs=pl.BlockSpec((tt, D), lambda i: (i, 0)),
        out_shape=jax.ShapeDtypeStruct((T, D), F32),
        compiler_params=_cparams(("parallel",)),
        name="glu",
    )(x, g, mods, w1_b, b1)


CONV_HALO = 16
SUBLANES = 8


def _convout_kernel(ap_ref, ac_ref, an_ref, x_ref, mod_ref, dw_ref, dwb_ref, lg_ref, lb_ref, w2_ref, b2_ref,
                    o_ref, ext_ref, sh_ref):
    i = pl.program_id(0)
    pos0, slen = _seq_tile_info(i)
    ext_ref[...] = _with_halo(ap_ref, ac_ref, an_ref, CONV_HALO, pos0, slen)
    n_sh = ST + 2 * CONV_HALO - SUBLANES
    for r in range(1, SUBLANES):
        sh_ref[r - 1] = ext_ref[pl.ds(r, n_sh), :]
    pad = CONV_W // 2
    half = ST // 2
    parts = []
    for c0 in range(0, D, 128):
        cs = slice(c0, c0 + 128)
        halves = []
        for r0 in (0, half):
            acc = jnp.zeros((half, 128), F32)
            for k in range(CONV_W):
                off = CONV_HALO - pad + k
                r = off % SUBLANES
                src = ext_ref if r == 0 else sh_ref.at[r - 1]
                acc = acc + src[pl.ds(off - r + r0, half), cs] * dw_ref[k:k + 1, cs]
            halves.append(acc)
        parts.append(jnp.concatenate(halves, axis=0))
    a = jnp.concatenate(parts, axis=1) + dwb_ref[...]
    mu = jnp.mean(a, axis=-1, keepdims=True)
    var = jnp.mean(jnp.square(a - mu), axis=-1, keepdims=True)
    y = (a - mu) * lax.rsqrt(var + EPS) * lg_ref[...] + lb_ref[...]
    y = jax.nn.silu(y)
    out = jnp.dot(y.astype(BF16), w2_ref[...], preferred_element_type=F32) + b2_ref[...]
    o_ref[...] = x_ref[...] + mod_ref[2:3, :] * out


def _convout(a, x, mods, dw, dwb, lg, lb, w2_b, b2):
    row = lambda: pl.BlockSpec((1, D), lambda i: (0, 0))
    return pl.pallas_call(
        _convout_kernel,
        grid=(N_ST,),
        in_specs=_halo_specs(D) + [
            pl.BlockSpec((ST, D), lambda i: (i, 0)),
            pl.BlockSpec((None, 6, D), lambda i: (_group_of_tile(i, ST), 0, 0)),
            pl.BlockSpec((CONV_W, D), lambda i: (0, 0)),
            row(), row(), row(),
            pl.BlockSpec((D, D), lambda i: (0, 0)),
            row(),
        ],
        out_specs=pl.BlockSpec((ST, D), lambda i: (i, 0)),
        out_shape=jax.ShapeDtypeStruct((T, D), F32),
        scratch_shapes=[pltpu.VMEM((ST + 2 * CONV_HALO, D), F32),
                        pltpu.VMEM((SUBLANES - 1, ST + 2 * CONV_HALO - SUBLANES, D), F32)],
        compiler_params=_cparams(("parallel",)),
        name="convout",
    )(a, a, a, x, mods, dw, dwb, lg, lb, w2_b, b2)


SEL_TT = 256
LANES = 128


def _topk16_exact(problems):
    vals = [[] for _ in problems]
    for r in range(P_TOPK):
        for i, (work_ref, rank_ref) in enumerate(problems):
            shape = work_ref.shape
            iota = lax.broadcasted_iota(jnp.int32, shape, 0).astype(F32)
            s = work_ref[...]
            m = jnp.max(s, axis=0, keepdims=True)
            hit = iota == jnp.min(jnp.where(s == m, iota, float(shape[0])), axis=0, keepdims=True)
            pltpu.store(work_ref, jnp.full(shape, -INF, F32), mask=hit)
            pltpu.store(rank_ref, jnp.full(shape, float(r), F32), mask=hit)
            vals[i].append(m)
    return [jnp.concatenate(v, axis=0) for v in vals]


def _compare_exchange(t, hi, lo):
    if t[lo] is None:
        return
    if t[hi] is None:
        t[hi], t[lo] = t[lo], None
        return
    t[hi], t[lo] = jnp.maximum(t[hi], t[lo]), jnp.minimum(t[hi], t[lo])


def _top16_sorted(scores):
    n = P_TOPK
    n_tiles = scores.shape[0] // SUBLANES
    t = [scores[SUBLANES * i:SUBLANES * (i + 1), :] if i < n_tiles else None for i in range(n)]
    k = 2
    while k <= n:
        j = k // 2
        while j >= 1:
            for i in range(n):
                m = i ^ j
                if m > i:
                    if i & k == 0:
                        _compare_exchange(t, i, m)
                    else:
                        _compare_exchange(t, m, i)
            j //= 2
        k *= 2
    for shift in (4, 2, 1):
        other = [None if x is None else pltpu.roll(x, shift, 0) for x in t]

        def larger(a, b):
            return b if a is None else a if b is None else jnp.maximum(a, b)

        t = [larger(t[i], other[n - 1 - i]) for i in range(n)]
        j = n // 2
        while j >= 1:
            for i in range(n):
                if i ^ j > i:
                    _compare_exchange(t, i, i ^ j)
            j //= 2
    return jnp.concatenate([x[0:1, :] for x in t], axis=0)


def _count_ge(s, thr):
    return jnp.sum((s >= thr).astype(F32), axis=0, keepdims=True)


_CAND_BLOCKS = [(0, 0, 8), (0, 8, 8)] + [(r1, 0, P_TOPK // (r1 + 1)) for r1 in range(1, 8)] + [(None, 0, 8)]


N_CAND = 8 * len(_CAND_BLOCKS)
HEADS_PER_TRIP = 4


def _select_kernel(x_ref, g_ref, mod_ref, wq_ref, keys_ref, hb_ref, r_ref, e2_ref, c_ref, e1_ref,
                   q_scr, s_scr, wk_scr, rk_scr, cw_scr, cr_scr, ce_scr):
    h = _modnorm(x_ref[...], g_ref[...], mod_ref[3:4, :], mod_ref[4:5, :])
    hb = h.astype(BF16)
    hb_ref[...] = h.T.astype(BF16)
    q_scr[...] = jnp.dot(hb, wq_ref[...], preferred_element_type=F32).astype(BF16)
    iota8 = lax.broadcasted_iota(jnp.int32, (8, LANES), 0)

    n_lc = SEL_TT // LANES

    def head_group(i, carry):
        for j in range(HEADS_PER_TRIP):
            hd = i * HEADS_PER_TRIP + j
            for p in range(2):
                col = pl.multiple_of((hd * 2 + p) * N_KEYS, N_KEYS)
                s_scr[j, p] = lax.dot_general(keys_ref[hd * 2 + p], q_scr[:, pl.ds(col, N_KEYS)], _NT,
                                              preferred_element_type=F32)
        tied = select_group(i, exact=False)

        @pl.when(jnp.max(tied) > 0.0)
        def _():
            select_group(i, exact=True)

        return carry

    def select_group(i, exact):
        probs = [(i * HEADS_PER_TRIP + j, j, lc, j * n_lc + lc) for j in range(HEADS_PER_TRIP) for lc in range(n_lc)]
        tied = jnp.zeros((1, LANES), F32)
        for hd, j, lc, sc in probs:
            ls = slice(lc * LANES, (lc + 1) * LANES)
            for p in range(2):
                wk_scr[sc, p] = s_scr[j, p, :, ls]
                rk_scr[sc, p] = jnp.full((N_KEYS, LANES), float(P_TOPK), F32)
        if exact:
            tops = _topk16_exact([(wk_scr.at[sc, p], rk_scr.at[sc, p]) for _, _, _, sc in probs for p in range(2)])
        else:
            tops = [_top16_sorted(wk_scr[sc, p]) for _, _, _, sc in probs for p in range(2)]
        for n, (hd, j, lc, sc) in enumerate(probs):
            v1, v2 = tops[2 * n], tops[2 * n + 1]
            for b, (r1, r2, nvalid) in enumerate(_CAND_BLOCKS):
                if r1 is None:
                    blk = v1[8:16, :] + v2[0:1, :]
                else:
                    blk = v1[r1:r1 + 1, :] + v2[r2:r2 + 8, :]
                if nvalid < 8:
                    blk = jnp.where(iota8 < nvalid, blk, -INF)
                cw_scr[sc, 8 * b:8 * b + 8, :] = blk
                ce_scr[sc, 8 * b:8 * b + 8, :] = jnp.exp(blk - (v1[0:1, :] + v2[0:1, :]))
            if exact:
                cr_scr[sc] = jnp.full((N_CAND, LANES), float(P_TOPK), F32)
        if exact:
            _topk16_exact([(cw_scr.at[sc], cr_scr.at[sc]) for _, _, _, sc in probs])
        else:
            ctops = [_top16_sorted(cw_scr[sc]) for _, _, _, sc in probs]
        for n, (hd, j, lc, sc) in enumerate(probs):
            ls = slice(lc * LANES, (lc + 1) * LANES)
            v1, v2 = tops[2 * n], tops[2 * n + 1]
            s1 = s_scr[j, 0, :, ls]
            s2 = s_scr[j, 1, :, ls]
            if exact:
                rank1 = rk_scr[sc, 0]
                sel = cr_scr[sc] < float(P_TOPK)
            else:
                sel = cw_scr[sc] >= ctops[n][P_TOPK - 1:P_TOPK, :]
                for cnt_ge in (_count_ge(s1, v1[P_TOPK - 1:P_TOPK, :]), _count_ge(s2, v2[P_TOPK - 1:P_TOPK, :]),
                               jnp.sum(sel.astype(F32), axis=0, keepdims=True)):
                    tied = jnp.maximum(tied, (cnt_ge != float(P_TOPK)).astype(F32))
                for v in (v1, v2):
                    repeats = (v[:P_TOPK - 1, :] == v[1:, :]).astype(F32)
                    tied = jnp.maximum(tied, jnp.max(repeats, axis=0, keepdims=True))
            self32 = sel.astype(F32)
            z = jnp.sum(jnp.where(sel, ce_scr[sc], 0.0), axis=0, keepdims=True)
            cnt_rows = [jnp.sum(self32[0:16, :], axis=0, keepdims=True)]
            for b in range(2, 9):
                cnt_rows.append(jnp.sum(self32[8 * b:8 * b + 8, :], axis=0, keepdims=True))
            cnt = jnp.concatenate(cnt_rows + [self32[72:80, :]], axis=0)
            if exact:
                c_ref[hd, lc] = jnp.zeros((N_KEYS, LANES), F32)
                for r in range(P_TOPK):
                    pltpu.store(c_ref.at[hd, lc], jnp.broadcast_to(cnt[r:r + 1, :], (N_KEYS, LANES)),
                                mask=rank1 == float(r))
                rank2 = rk_scr[sc, 1]
            else:
                kept = jnp.zeros((N_KEYS, LANES), F32)
                rank2 = jnp.full((N_KEYS, LANES), float(P_TOPK), F32)
                for r in range(P_TOPK):
                    kept = jnp.where(s1 == v1[r:r + 1, :], cnt[r:r + 1, :], kept)
                    rank2 = jnp.where(s2 == v2[r:r + 1, :], float(r), rank2)
                c_ref[hd, lc] = kept
            r_ref[hd, :, ls] = rank2.astype(BF16)
            e2_ref[hd, :, ls] = jnp.exp(s2 - v2[0:1, :]).astype(BF16)
            e1_ref[hd, lc] = jnp.exp(s1 - v1[0:1, :]) * (0.5 / z)
        return tied

    lax.fori_loop(0, P_HEADS // HEADS_PER_TRIP, head_group, 0)


def _select(x, g, mods, wq_b, keys_b):
    tt = SEL_TT
    hk = pl.BlockSpec((P_HEADS, N_KEYS, tt), lambda i: (0, 0, i))
    hs = pl.BlockSpec((P_HEADS, tt // LANES, N_KEYS, LANES), lambda i: (0, i, 0, 0))
    n_slot = HEADS_PER_TRIP * (tt // LANES)
    return pl.pallas_call(
        _select_kernel,
        grid=(T // tt,),
        in_specs=[
            pl.BlockSpec((tt, D), lambda i: (i, 0)),
            pl.BlockSpec((1, D), lambda i: (0, 0)),
            pl.BlockSpec((None, 6, D), lambda i: (_group_of_tile(i, tt), 0, 0)),
            pl.BlockSpec((D, 2 * P_HEADS * N_KEYS), lambda i: (0, 0)),
            pl.BlockSpec((2 * P_HEADS, N_KEYS, N_KEYS), lambda i: (0, 0, 0)),
        ],
        out_specs=[pl.BlockSpec((D, tt), lambda i: (0, i)), hk, hk, hs, hs],
        out_shape=[
            jax.ShapeDtypeStruct((D, T), BF16),
            jax.ShapeDtypeStruct((P_HEADS, N_KEYS, T), BF16),
            jax.ShapeDtypeStruct((P_HEADS, N_KEYS, T), BF16),
            jax.ShapeDtypeStruct((P_HEADS, T // LANES, N_KEYS, LANES), F32),
            jax.ShapeDtypeStruct((P_HEADS, T // LANES, N_KEYS, LANES), F32),
        ],
        scratch_shapes=[
            pltpu.VMEM((tt, 2 * P_HEADS * N_KEYS), BF16),
            pltpu.VMEM((HEADS_PER_TRIP, 2, N_KEYS, tt), F32),
            pltpu.VMEM((n_slot, 2, N_KEYS, LANES), F32),
            pltpu.VMEM((n_slot, 2, N_KEYS, LANES), F32),
            pltpu.VMEM((n_slot, N_CAND, LANES), F32),
            pltpu.VMEM((n_slot, N_CAND, LANES), F32),
            pltpu.VMEM((n_slot, N_CAND, LANES), F32),
        ],
        compiler_params=_cparams(("parallel",)),
        name="peer_select",
    )(x, g, mods, wq_b, keys_b)


DT = 512
DE = 2048
I1_PER = DE // N_KEYS
PACK = 16
DSUB = 256


def _row_replicated(ref, hd, sub, ii):
    slabs = [jnp.broadcast_to(ref[hd, sub * (DSUB // LANES) + s, ii:ii + 1, :], (PACK, LANES))
             for s in range(DSUB // LANES)]
    return jnp.concatenate(slabs, axis=1)


def _dense_kernel(final_norm, hb_ref, u_ref, vt_ref, r_ref, e2_ref, c_ref, e1_ref, x_ref, mod_ref, fg_ref, *refs):
    *out_refs, acc_ref, p_ref = refs
    k = pl.program_id(1)

    @pl.when(k == 0)
    def _():
        acc_ref[...] = jnp.zeros_like(acc_ref)

    def scores(sub):
        return jnp.dot(u_ref[...], hb_ref[:, sub * DSUB:(sub + 1) * DSUB], preferred_element_type=F32)

    n_sub = DT // DSUB
    a_next = scores(0)
    for sub in range(n_sub):
        ts = slice(sub * DSUB, (sub + 1) * DSUB)
        a_t = a_next
        if sub + 1 < n_sub:
            a_next = scores(sub + 1)
        act = (a_t * (1.0 + lax.erf(a_t * math.sqrt(0.5)))).astype(BF16)
        for ii in range(I1_PER):
            w = jnp.zeros((N_KEYS // PACK, PACK, DSUB), BF16)
            for hd in range(P_HEADS):
                cb = _row_replicated(c_ref, hd, sub, ii).astype(BF16)
                eb = _row_replicated(e1_ref, hd, sub, ii).astype(BF16)
                w = w + jnp.where(r_ref[hd, :, :, ts] < cb[None], e2_ref[hd, :, :, ts], jnp.zeros((), BF16)) * eb[None]
            rows = slice(ii * N_KEYS, (ii + 1) * N_KEYS)
            p_ref[rows, ts] = act[rows, :] * w.reshape(N_KEYS, DSUB)
        acc_ref[:, ts] += jnp.dot(vt_ref[...], p_ref[:, ts], preferred_element_type=F32)

    @pl.when(k == pl.num_programs(1) - 1)
    def _():
        y = x_ref[...] + mod_ref[5:6, :] * acc_ref[...].T
        if not final_norm:
            out_refs[0][...] = y
        else:
            y = y * lax.rsqrt(jnp.mean(y * y, axis=-1, keepdims=True) + EPS) * fg_ref[...]
            is_ctx = pl.program_id(0) < T_CTX // DT

            @pl.when(is_ctx)
            def _():
                out_refs[0][...] = y

            @pl.when(jnp.logical_not(is_ctx))
            def _():
                out_refs[1][...] = y


def _dense(layer, hb, u_tab, vt_b, rk, e2, c, e1, x, mods, fg, final_norm):
    hk_b = pl.BlockSpec((P_HEADS, N_KEYS // PACK, PACK, DT), lambda j, k: (0, 0, 0, j))
    hk_f = pl.BlockSpec((P_HEADS, DT // LANES, I1_PER, LANES), lambda j, k: (0, j, k, 0))
    rk4 = rk.reshape(P_HEADS, N_KEYS // PACK, PACK, T)
    e24 = e2.reshape(P_HEADS, N_KEYS // PACK, PACK, T)
    if final_norm:
        n_ctx = T_CTX // DT
        out_specs = [pl.BlockSpec((DT, D), lambda j, k: (jnp.minimum(j, n_ctx - 1), 0), pipeline_mode=pl.Buffered(1)),
                     pl.BlockSpec((DT, D), lambda j, k: (jnp.maximum(j - n_ctx, 0), 0), pipeline_mode=pl.Buffered(1))]
        out_shape = [jax.ShapeDtypeStruct((T_CTX, D), F32), jax.ShapeDtypeStruct((T_LAT, D), F32)]
    else:
        out_specs = pl.BlockSpec((DT, D), lambda j, k: (j, 0))
        out_shape = jax.ShapeDtypeStruct((T, D), F32)
    return pl.pallas_call(
        functools.partial(_dense_kernel, final_norm),
        grid=(T // DT, N_EXPERTS // DE),
        in_specs=[
            pl.BlockSpec((D, DT), lambda j, k: (0, j)),
            pl.BlockSpec((None, DE, D), lambda j, k: (layer, k, 0)),
            pl.BlockSpec((None, D, DE), lambda j, k: (layer, 0, k)),
            hk_b, hk_b, hk_f, hk_f,
            pl.BlockSpec((DT, D), lambda j, k: (j, 0), pipeline_mode=pl.Buffered(1)),
            pl.BlockSpec((None, 6, D), lambda j, k: (_group_of_tile(j, DT), 0, 0)),
            pl.BlockSpec((1, D), lambda j, k: (0, 0)),
        ],
        out_specs=out_specs,
        out_shape=out_shape,
        scratch_shapes=[pltpu.VMEM((D, DT), F32), pltpu.VMEM((DE, DT), BF16)],
        compiler_params=_cparams(("arbitrary", "arbitrary")),
        name="peer_dense",
    )(hb, u_tab, vt_b, rk4, e24, c, e1, x, mods, fg)


def _transpose_cast_kernel(v_ref, o_ref):
    o_ref[...] = v_ref[...].T


def _value_tables(peer_v):
    n_layers = peer_v.shape[0]
    blk = 1024
    return pl.pallas_call(
        _transpose_cast_kernel,
        grid=(n_layers, N_EXPERTS // blk),
        in_specs=[pl.BlockSpec((None, blk, D), lambda l, e: (l, e, 0))],
        out_specs=pl.BlockSpec((None, D, blk), lambda l, e: (l, 0, e)),
        out_shape=jax.ShapeDtypeStruct((n_layers, D, N_EXPERTS), F32),
        compiler_params=_cparams(("parallel", "parallel")),
        name="value_tables",
    )(peer_v)


def _peer(layer, x, g, mods, wq, keys, u_tab, vt_b, fg, final_norm):
    hb, rk, e2, c, e1 = _select(x, g, mods, wq.astype(BF16),
                                keys.reshape(2 * P_HEADS, N_KEYS, N_KEYS).astype(BF16))
    return _dense(layer, hb, u_tab, vt_b, rk, e2, c, e1, x, mods, fg, final_norm)


def kernel(x_prompt, x_sample, c, cache_k, cache_v, c_ctx, mod_w, mod_b, norm_mix_g, norm_ffn_g, w_in, attn_sink, w_pool, pool_scale, w_out, conv_w1, conv_b1, conv_dw, conv_dw_b, conv_ln_g, conv_ln_b, conv_w2, conv_b2, peer_wq, peer_keys, peer_u, peer_v, final_norm_g):
    x_ctx, x_lat = x_prompt.reshape(T_CTX, D), x_sample.reshape(T_LAT, D)
    cvec = jnp.concatenate([c_ctx[None, :], c, jnp.zeros((N_GROUPS - 1 - N_LAT_SEQ, D), F32)], axis=0)
    mods_all = _adaln(cvec, mod_w, mod_b).reshape(mod_w.shape[0], N_GROUPS, 6, D)
    row = lambda a: a.reshape(1, -1)
    fg = row(final_norm_g)

    mods = mods_all[0]
    q, k, v, u = _inproj(x_ctx, x_lat, row(norm_mix_g[0]), mods, w_in[0].astype(BF16))
    att_ctx = _ctx_attn(attn_sink[0], q, k, v)
    n_past = cache_k.shape[2]
    att_lat = _lat_attn(attn_sink[0], q, k, v,
                        cache_k[:, 0].reshape(N_LAT_SEQ, n_past, KV_DIM),
                        cache_v[:, 0].reshape(N_LAT_SEQ, n_past, KV_DIM))
    x = _poolout(att_ctx, att_lat, u, x_ctx, x_lat, mods, w_pool[0].astype(BF16), row(pool_scale[0]), w_out[0].astype(BF16))
    vt_b = _value_tables(peer_v)
    x = _peer(0, x, row(norm_ffn_g[0]), mods, peer_wq[0], peer_keys[0], peer_u, vt_b, fg, False)
    state_k = k[:T_CTX].reshape(N_CTX_SEQ, 1, SEQ_CTX, N_KV, HD)
    state_v = v[:T_CTX].reshape(N_CTX_SEQ, 1, SEQ_CTX, N_KV, HD)

    mods = mods_all[1]
    a = _glu(x, row(norm_mix_g[1]), mods, conv_w1[0].astype(BF16), row(conv_b1[0]))
    x = _convout(a, x, mods, conv_dw[0], row(conv_dw_b[0]), row(conv_ln_g[0]), row(conv_ln_b[0]),
                 conv_w2[0].astype(BF16), row(conv_b2[0]))
    y_ctx, y_lat = _peer(1, x, row(norm_ffn_g[1]), mods, peer_wq[1], peer_keys[1], peer_u, vt_b, fg, True)

    y_prompt = y_ctx.reshape(N_CTX_SEQ, SEQ_CTX, D)
    y_sample = y_lat.reshape(N_LAT_SEQ, SEQ_LAT, D)
    return (y_prompt, y_sample, state_k, state_v)
```
